```python
import math
import jax
import jax.numpy as jnp
from jax import lax
import numpy as np

D_MODEL = 1024
BATCH = 16
SEQ = 2048
DEPTH = 2

CTX_LEN = 256
GRID_W = 64
EPS = 1e-6
N_BRANCH = 4
BRANCH_W = 512
Q_BLOCK = 128
ROPE_BASE = 10000.0

MLA_HEADS = 8
MLA_NOPE = 64
MLA_ROPE = 32
MLA_V = 64
MLA_Q_RANK = 256
MLA_KV_RANK = 128

NA_HEADS = 8
NA_HEAD_DIM = 64
NA_WIN_R = 8
NA_WIN_C = 16

HY_WIDTH = 512
HY_ORDER = 2
HY_SHORT = 3
HY_EMB = 33
HY_FFN = 64
HY_TARGET = 1e-2
HY_FAST_DECAY_PCT = 0.3
HY_SLOW_DECAY_PCT = 1.5

GLA_HEADS = 4
GLA_DK = 64
GLA_DV = 128
GLA_GATE_RANK = 16
GLA_TAU = 16.0
GLA_CHUNK = 64

IN_LAYOUT = (
    ('mla_cq', MLA_Q_RANK),
    ('mla_ckv', MLA_KV_RANK),
    ('mla_kr', MLA_ROPE),
    ('mla_z', MLA_HEADS * MLA_V),
    ('na_qkv', 3 * NA_HEADS * NA_HEAD_DIM),
    ('na_z', NA_HEADS * NA_HEAD_DIM),
    ('hy_proj', (HY_ORDER + 1) * HY_WIDTH),
    ('hy_z', HY_WIDTH),
    ('gla_qk', 2 * GLA_HEADS * GLA_DK),
    ('gla_v', GLA_HEADS * GLA_DV),
    ('gla_glr', 2 * GLA_GATE_RANK),
    ('gla_z', GLA_HEADS * GLA_DV),
    ('merge', N_BRANCH * D_MODEL),
)
IN_TOTAL = (MLA_Q_RANK + MLA_KV_RANK + MLA_ROPE + MLA_HEADS * MLA_V
            + 4 * NA_HEADS * NA_HEAD_DIM + (HY_ORDER + 2) * HY_WIDTH
            + 2 * GLA_HEADS * GLA_DK + 2 * GLA_HEADS * GLA_DV + 2 * GLA_GATE_RANK
            + N_BRANCH * D_MODEL)
Z_NAMES = ('mla_z', 'na_z', 'hy_z', 'gla_z')

kernel_name = 'hybrid_parallel_mla_na_hyena_gla'


def _col_range(name):
    start = 0
    for n, w in IN_LAYOUT:
        if n == name:
            return start, start + w
        start += w
    raise KeyError(name)


def in_proj(hh, w_in, name):
    a, b = _col_range(name)
    return hh @ w_in[:, a:b]


def rms_norm(x, g):
    xf = x.astype(jnp.float32)
    xf = xf * lax.rsqrt(jnp.mean(jnp.square(xf), axis=-1, keepdims=True) + EPS)
    return (xf * g.astype(jnp.float32)).astype(x.dtype)


def axial_rope(x):
    L, R = x.shape[1], x.shape[-1]
    t = jnp.arange(L)
    row = (t // GRID_W).astype(jnp.float32)
    col = (t % GRID_W).astype(jnp.float32)
    half = R // 2
    inv = jnp.power(ROPE_BASE, -jnp.arange(0, half, 2, dtype=jnp.float32) / half)
    ar, ac = row[:, None] * inv, col[:, None] * inv
    ang = jnp.concatenate([ar, ar, ac, ac], axis=-1)
    shape = (1, L) + (1,) * (x.ndim - 3) + (R,)
    cos = jnp.cos(ang).reshape(shape).astype(x.dtype)
    sin = jnp.sin(ang).reshape(shape).astype(x.dtype)
    a1, b1, a2, b2 = jnp.split(x, 4, axis=-1)
    rx = jnp.concatenate([-b1, a1, -b2, a2], axis=-1)
    return x * cos + rx * sin


def block_attention(q, k, v, scale):
    B, S, H, dq = q.shape
    nb = S // Q_BLOCK
    qb = q.reshape(B, nb, Q_BLOCK, H, dq).transpose(1, 0, 2, 3, 4)

    def one(qblk):
        s = jnp.einsum('bqhd,bkhd->bhqk', qblk, k).astype(jnp.float32) * scale
        p = jax.nn.softmax(s, axis=-1).astype(v.dtype)
        return jnp.einsum('bhqk,bkhd->bqhd', p, v)

    o = lax.map(one, qb)
    return o.transpose(1, 0, 2, 3, 4).reshape(B, S, H * v.shape[-1])


def mla_branch(h, hc, w_in, q_norm, w_uq, kv_norm, w_ukv, need_ctx):
    def queries(hh, rotate):
        B, L, _ = hh.shape
        cq = rms_norm(in_proj(hh, w_in, 'mla_cq'), q_norm)
        q = (cq @ w_uq).reshape(B, L, MLA_HEADS, MLA_NOPE + MLA_ROPE)
        q_nope, q_rot = q[..., :MLA_NOPE], q[..., MLA_NOPE:]
        if rotate:
            q_rot = axial_rope(q_rot)
        return jnp.concatenate([q_nope, q_rot], axis=-1)

    def keys_values(hh, rotate):
        B, L, _ = hh.shape
        ckv = rms_norm(in_proj(hh, w_in, 'mla_ckv'), kv_norm)
        kv = (ckv @ w_ukv).reshape(B, L, MLA_HEADS, MLA_NOPE + MLA_V)
        k_nope, v = kv[..., :MLA_NOPE], kv[..., MLA_NOPE:]
        k_rot = in_proj(hh, w_in, 'mla_kr')[:, :, None, :]
        if rotate:
            k_rot = axial_rope(k_rot)
        k_rot = jnp.broadcast_to(k_rot, (B, L, MLA_HEADS, MLA_ROPE))
        return jnp.concatenate([k_nope, k_rot], axis=-1), v

    scale = (MLA_NOPE + MLA_ROPE) ** -0.5
    kc, vc = keys_values(hc, False)
    k, v = keys_values(h, True)
    y = block_attention(queries(h, True), jnp.concatenate([kc, k], axis=1),
                        jnp.concatenate([vc, v], axis=1), scale)
    yc = block_attention(queries(hc, False), kc, vc, scale) if need_ctx else None
    return y, yc


def neighbourhood_attention(q, k, v, kc, vc, rpb, scale):
    B, S, H, d = q.shape
    rows = S // GRID_W
    wr = min(NA_WIN_R, rows)
    qg = q.reshape(B, rows, GRID_W, H, d)
    kg = k.reshape(B, rows, GRID_W, H, d)
    vg = v.reshape(B, rows, GRID_W, H, d)
    j = jnp.arange(GRID_W)
    c0 = jnp.clip(j - NA_WIN_C // 2, 0, GRID_W - NA_WIN_C)
    col_mask = (j[None, :] >= c0[:, None]) & (j[None, :] < c0[:, None] + NA_WIN_C)
    dcol = jnp.clip(j[None, :] - j[:, None], -(NA_WIN_C - 1), NA_WIN_C - 1) + NA_WIN_C - 1
    rpb = rpb.astype(jnp.float32)
    n_win = wr * GRID_W

    def one_row(r):
        r0 = jnp.clip(r - wr // 2, 0, rows - wr)
        kb = lax.dynamic_slice_in_dim(kg, r0, wr, axis=1)
        vb = lax.dynamic_slice_in_dim(vg, r0, wr, axis=1)
        qr = lax.dynamic_index_in_dim(qg, r, axis=1, keepdims=False)
        drow = r0 + jnp.arange(wr) - r + NA_WIN_R - 1
        bias = rpb[:, drow][:, :, dcol].transpose(0, 2, 1, 3)
        s_win = jnp.einsum('bqhd,brkhd->bhqrk', qr, kb).astype(jnp.float32) * scale + bias
        s_win = jnp.where(col_mask[:, None, :], s_win, -jnp.inf)
        s_ctx = jnp.einsum('bqhd,bkhd->bhqk', qr, kc).astype(jnp.float32) * scale
        s = jnp.concatenate([s_win.reshape(B, H, GRID_W, n_win), s_ctx], axis=-1)
        p = jax.nn.softmax(s, axis=-1).astype(v.dtype)
        p_win = p[..., :n_win].reshape(B, H, GRID_W, wr, GRID_W)
        return (jnp.einsum('bhqrk,brkhd->bqhd', p_win, vb)
                + jnp.einsum('bhqk,bkhd->bqhd', p[..., n_win:], vc))

    o = lax.map(one_row, jnp.arange(rows))
    return o.transpose(1, 0, 2, 3, 4).reshape(B, S, H * d)


def na_branch(h, hc, w_in, rpb, need_ctx):
    def qkv(hh):
        B, L, _ = hh.shape
        u = in_proj(hh, w_in, 'na_qkv').reshape(B, L, 3, NA_HEADS, NA_HEAD_DIM)
        return u[:, :, 0], u[:, :, 1], u[:, :, 2]

    scale = NA_HEAD_DIM ** -0.5
    qc, kc, vc = qkv(hc)
    q, k, v = qkv(h)
    y = neighbourhood_attention(q, k, v, kc, vc, rpb, scale)
    yc = block_attention(qc, kc, vc, scale) if need_ctx else None
    return y, yc


def short_conv(u, w, b):
    y = lax.conv_general_dilated(u, w[:, None, :].astype(u.dtype), window_strides=(1,),
                                 padding=((HY_SHORT // 2, HY_SHORT // 2),),
                                 dimension_numbers=('NWC', 'WIO', 'NWC'),
                                 feature_group_count=u.shape[-1])
    return y + b.astype(u.dtype)


def hyena_filters(L, w1, b1, freq, w2, b2, w3):
    f32 = jnp.float32
    t = jnp.arange(L, dtype=f32)
    t_norm = t / max(L - 1, 1)
    bands = (HY_EMB - 1) // 2
    fr = jnp.linspace(1e-4, bands - 1, bands, dtype=f32)
    ang = (2.0 * math.pi / L) * t[:, None] * fr[None, :]
    z = jnp.concatenate([t_norm[:, None], jnp.cos(ang), -jnp.sin(ang)], axis=-1)
    a = jnp.sin(freq[0].astype(f32) * (z @ w1.astype(f32) + b1.astype(f32)))
    a = jnp.sin(freq[1].astype(f32) * (a @ w2.astype(f32) + b2.astype(f32)))
    filt = (a @ w3.astype(f32)).reshape(L, HY_ORDER, 2, HY_WIDTH)
    max_decay = math.log(HY_TARGET) / HY_FAST_DECAY_PCT
    min_decay = math.log(HY_TARGET) / HY_SLOW_DECAY_PCT
    deltas = jnp.abs(jnp.linspace(min_decay, max_decay, HY_WIDTH, dtype=f32))
    window = jnp.exp(-t_norm[:, None] * deltas[None, :])
    filt = filt * window[:, None, None, :]
    return filt / (jnp.sum(jnp.abs(filt), axis=(0, 2), keepdims=True) + EPS)


def long_conv_bidir(u, h_fwd, h_bwd, skip):
    L, C = u.shape[1], u.shape[2]
    filt2 = jnp.concatenate([h_fwd, jnp.zeros((1, C), jnp.float32), jnp.flip(h_bwd[1:], axis=0)], axis=0)
    uf = u.astype(jnp.float32)
    y = jnp.fft.irfft(jnp.fft.rfft(uf, n=2 * L, axis=1) * jnp.fft.rfft(filt2, axis=0)[None],
                      n=2 * L, axis=1)[:, :L]
    return (y + uf * skip.astype(jnp.float32)).astype(u.dtype)


def hyena_branch(h, hc, w_in, conv_w, conv_b, pe_w1, pe_b1, pe_freq, pe_w2, pe_b2, pe_w3, skip, need_ctx):
    def run(hh):
        L = hh.shape[1]
        u = short_conv(in_proj(hh, w_in, 'hy_proj'), conv_w, conv_b)
        x1, x2, z = jnp.split(u, HY_ORDER + 1, axis=-1)
        filt = hyena_filters(L, pe_w1, pe_b1, pe_freq, pe_w2, pe_b2, pe_w3)
        for n, gate in enumerate((x1, x2)):
            z = gate * long_conv_bidir(z, filt[:, n, 0], filt[:, n, 1], skip[n])
        return z

    return run(h), (run(hc) if need_ctx else None)


def gla_chunked(q, k, v, log_a, s0):
    B, L, H, DK = q.shape
    DV = v.shape[-1]
    C = GLA_CHUNK
    n = L // C

    def chunks(t):
        return t.reshape(B, n, C, H, t.shape[-1]).transpose(1, 0, 3, 2, 4)

    q, k, v, g = chunks(q), chunks(k), chunks(v), chunks(log_a)
    b = jnp.cumsum(g, axis=3)
    b_last = b[:, :, :, -1:]
    b_mid = b[:, :, :, C // 2 - 1:C // 2]
    a = jnp.einsum('nbhtd,nbhsd->nbhts', q * jnp.exp(b - b_mid), k * jnp.exp(b_mid - b))
    a = jnp.where(jnp.tril(jnp.ones((C, C), dtype=bool)), a, 0.0)
    o_intra = jnp.einsum('nbhts,nbhsv->nbhtv', a, v)
    q_in = q * jnp.exp(b)
    k_out = k * jnp.exp(b_last - b)
    decay = jnp.exp(b_last[:, :, :, 0])

    def step(s, inp):
        qi, ko, vi, dec = inp
        o = jnp.einsum('bhtd,bhdv->bhtv', qi, s)
        s = dec[..., None] * s + jnp.einsum('bhtd,bhtv->bhdv', ko, vi)
        return s, o

    s_final, o_inter = lax.scan(step, s0, (q_in, k_out, v, decay))
    o = (o_intra + o_inter).transpose(1, 0, 3, 2, 4).reshape(B, L, H, DV)
    return o, s_final


def gla_branch(h, hc, w_in, wg2, bg, norm_g, need_ctx):
    f32 = jnp.float32

    def feats(hh):
        B, L, _ = hh.shape
        q, k = jnp.split(in_proj(hh, w_in, 'gla_qk').astype(f32), 2, axis=-1)
        q = q.reshape(B, L, GLA_HEADS, GLA_DK) * GLA_DK ** -0.5
        k = k.reshape(B, L, GLA_HEADS, GLA_DK)
        v = in_proj(hh, w_in, 'gla_v').astype(f32).reshape(B, L, GLA_HEADS, GLA_DV)
        glr = in_proj(hh, w_in, 'gla_glr').astype(f32).reshape(B, L, 2, GLA_GATE_RANK)
        logit = jnp.einsum('blir,ire->blie', glr, wg2.astype(f32)) + bg.astype(f32)
        log_a = (jax.nn.log_sigmoid(logit) / GLA_TAU).reshape(B, L, 2, GLA_HEADS, GLA_DK)
        return q, k, v, log_a

    def flip(t):
        return jnp.flip(t, axis=1)

    def bidir(q, k, v, log_a, s_fwd, s_bwd):
        o_f, s_f = gla_chunked(q, k, v, log_a[:, :, 0], s_fwd)
        o_b, s_b = gla_chunked(flip(q), flip(k), flip(v), flip(log_a[:, :, 1]), s_bwd)
        return o_f + flip(o_b), s_f, s_b

    def finish(o, like):
        B, L = o.shape[:2]
        return rms_norm(o, norm_g).reshape(B, L, GLA_HEADS * GLA_DV).astype(like.dtype)

    s0 = jnp.zeros((h.shape[0], GLA_HEADS, GLA_DK, GLA_DV), f32)
    qc, kc, vc, lac = feats(hc)
    o_c, s_f, s_b = bidir(qc, kc, vc, lac, s0, s0)
    q, k, v, la = feats(h)
    o, _, _ = bidir(q, k, v, la, s_f, s_b)
    return finish(o, h), (finish(o_c, hc) if need_ctx else None)


def merge_branches(hh, ys, w_in, w_branch, w_out):
    B, L, _ = hh.shape
    gates = jax.nn.sigmoid(in_proj(hh, w_in, 'merge').astype(jnp.float32)).astype(hh.dtype)
    gates = gates.reshape(B, L, N_BRANCH, D_MODEL)
    terms = [gates[:, :, i] * ((y * jax.nn.silu(in_proj(hh, w_in, zname))) @ w_branch[i])
             for i, (y, zname) in enumerate(zip(ys, Z_NAMES))]
    merged = sum(terms[1:], terms[0])
    return merged @ w_out


def setup_inputs(seed: int = 0) -> dict:
    key = jax.random.key(seed)
    ks = iter(jax.random.split(key, 32))
    f32 = jnp.float32
    D = D_MODEL

    def nrm(shape, scale):
        return jax.random.normal(next(ks), shape, f32) * scale

    def gain(shape):
        return 1.0 + nrm(shape, 0.05)

    return {
        'x': nrm((BATCH, SEQ, D), 1.0),
        'c': nrm((BATCH, D), 1.0),
        'ctx': nrm((BATCH, CTX_LEN, D), 1.0),
        'c_ctx': nrm((D,), 1.0),
        'ada_w': nrm((DEPTH, D, 3 * D), 0.5 * D ** -0.5),
        'ada_b': nrm((DEPTH, 3 * D), 0.01),
        'pre_g': gain((DEPTH, D)),
        'post_g': gain((DEPTH, D)),
        'w_in': nrm((DEPTH, D, IN_TOTAL), D ** -0.5),
        'mla_q_norm': gain((DEPTH, MLA_Q_RANK)),
        'mla_w_uq': nrm((DEPTH, MLA_Q_RANK, MLA_HEADS * (MLA_NOPE + MLA_ROPE)), MLA_Q_RANK ** -0.5),
        'mla_kv_norm': gain((DEPTH, MLA_KV_RANK)),
        'mla_w_ukv': nrm((DEPTH, MLA_KV_RANK, MLA_HEADS * (MLA_NOPE + MLA_V)), MLA_KV_RANK ** -0.5),
        'na_rpb': nrm((DEPTH, NA_HEADS, 2 * NA_WIN_R - 1, 2 * NA_WIN_C - 1), 0.1),
        'hy_conv_w': nrm((DEPTH, HY_SHORT, (HY_ORDER + 1) * HY_WIDTH), HY_SHORT ** -0.5),
        'hy_conv_b': nrm((DEPTH, (HY_ORDER + 1) * HY_WIDTH), 0.01),
        'hy_pe_w1': nrm((DEPTH, HY_EMB, HY_FFN), HY_EMB ** -0.5),
        'hy_pe_b1': nrm((DEPTH, HY_FFN), 0.1),
        'hy_pe_freq': gain((DEPTH, 2, HY_FFN)),
        'hy_pe_w2': nrm((DEPTH, HY_FFN, HY_FFN), HY_FFN ** -0.5),
        'hy_pe_b2': nrm((DEPTH, HY_FFN), 0.1),
        'hy_pe_w3': nrm((DEPTH, HY_FFN, HY_ORDER * 2 * HY_WIDTH), HY_FFN ** -0.5),
        'hy_skip': nrm((DEPTH, HY_ORDER, HY_WIDTH), 1.0),
        'gla_wg2': nrm((DEPTH, 2, GLA_GATE_RANK, GLA_HEADS * GLA_DK), GLA_GATE_RANK ** -0.5),
        'gla_bg': nrm((DEPTH, 2, GLA_HEADS * GLA_DK), 0.1),
        'gla_norm': gain((DEPTH, GLA_DV)),
        'w_branch': nrm((DEPTH, N_BRANCH, BRANCH_W, D), BRANCH_W ** -0.5),
        'w_out': nrm((DEPTH, D, D), D ** -0.5),
    }


def reference(x, c, ctx, c_ctx, ada_w, ada_b, pre_g, post_g, w_in, mla_q_norm, mla_w_uq,
              mla_kv_norm, mla_w_ukv, na_rpb, hy_conv_w, hy_conv_b, hy_pe_w1, hy_pe_b1,
              hy_pe_freq, hy_pe_w2, hy_pe_b2, hy_pe_w3, hy_skip, gla_wg2, gla_bg, gla_norm,
              w_branch, w_out):
    cx = ctx
    for l in range(DEPTH):
        need_ctx = l < DEPTH - 1
        shift, scale, gate = jnp.split(jax.nn.silu(c) @ ada_w[l] + ada_b[l], 3, axis=-1)
        shift_c, scale_c, gate_c = jnp.split(jax.nn.silu(c_ctx) @ ada_w[l] + ada_b[l], 3, axis=-1)
        h = rms_norm(x, pre_g[l]) * (1.0 + scale[:, None]) + shift[:, None]
        hc = rms_norm(cx, pre_g[l]) * (1.0 + scale_c) + shift_c
        w = w_in[l]
        ya, yca = mla_branch(h, hc, w, mla_q_norm[l], mla_w_uq[l], mla_kv_norm[l], mla_w_ukv[l], need_ctx)
        yb, ycb = na_branch(h, hc, w, na_rpb[l], need_ctx)
        yh, ych = hyena_branch(h, hc, w, hy_conv_w[l], hy_conv_b[l], hy_pe_w1[l], hy_pe_b1[l],
                               hy_pe_freq[l], hy_pe_w2[l], hy_pe_b2[l], hy_pe_w3[l], hy_skip[l], need_ctx)
        yg, ycg = gla_branch(h, hc, w, gla_wg2[l], gla_bg[l], gla_norm[l], need_ctx)
        out = merge_branches(h, (ya, yb, yh, yg), w, w_branch[l], w_out[l])
        if need_ctx:
            out_c = merge_branches(hc, (yca, ycb, ych, ycg), w, w_branch[l], w_out[l])
            cx = cx + gate_c * rms_norm(out_c, post_g[l])
        x = x + gate[:, None] * rms_norm(out, post_g[l])
    return x
```

```python
import functools
import math

import numpy as np
import jax
import jax.numpy as jnp
from jax import lax
from jax.experimental import pallas as pl
from jax.experimental.pallas import tpu as pltpu

F32 = jnp.float32
BF16 = jnp.bfloat16

D_MODEL = 1024
SEQ = 2048
CTX_LEN = 256
ROWS = SEQ + CTX_LEN
GRID_W = 64
GRID_ROWS = SEQ // GRID_W
EPS = 1e-6
ROPE_BASE = 10000.0
TILE = 256
N_LAT_TILES = SEQ // TILE
N_TILES = ROWS // TILE

MLA_HEADS = 8
MLA_NOPE = 64
MLA_ROPE = 32
MLA_V = 64
MLA_Q_RANK = 256
MLA_KV_RANK = 128
MLA_SLOT = 128

NA_HEADS = 8
NA_HEAD_DIM = 64
NA_WIN_R = 8
NA_WIN_C = 16
NA_GROUP = 4
NA_KROWS = NA_GROUP + NA_WIN_R - 1
NA_NEG = -1e30

HY_WIDTH = 512
HY_EMB = 33
HY_FFN = 64
HY_TARGET = 1e-2
HY_FAST_DECAY_PCT = 0.3
HY_SLOW_DECAY_PCT = 1.5
HY_CT = 256

GLA_HEADS = 4
GLA_DK = 64
GLA_DV = 128
GLA_RANK = 16
GLA_TAU = 16.0
GLA_CHUNK = 64
GLA_NCHUNK = ROWS // GLA_CHUNK
GLA_CTX_CHUNKS = CTX_LEN // GLA_CHUNK

OFF_MERGE = 0
OFF_MLA = 4096
OFF_MLA_Z = 4608
OFF_NA_Q = 5120
OFF_NA_K = 5632
OFF_NA_V = 6144
OFF_NA_Z = 6656
OFF_HY_X1 = 7168
OFF_HY_X2 = 7680
OFF_HY_ZIN = 8192
OFF_HY_Z = 8704
OFF_GLA_QK = 9216
OFF_GLA_V = 9728
OFF_GLA_Z = 10240
P_COLS = 10752
INPROJ_TN = 512

VMEM_LIMIT = 56 * 1024 * 1024

_REF_LAYOUT = (
    ('mla_cq', 256), ('mla_ckv', 128), ('mla_kr', 32), ('mla_z', 512), ('na_qkv', 1536),
    ('na_z', 512), ('hy_proj', 1536), ('hy_z', 512), ('gla_qk', 512), ('gla_v', 512),
    ('gla_glr', 32), ('gla_z', 512), ('merge', 4096),
)


def _ref_cols(name):
    start = 0
    for n, w in _REF_LAYOUT:
        if n == name:
            return start, start + w
        start += w
    raise KeyError(name)


def _params(sem, vmem=VMEM_LIMIT):
    return pltpu.CompilerParams(dimension_semantics=sem, vmem_limit_bytes=vmem)


def _single(shape, index_map):
    return pl.BlockSpec(shape, index_map, pipeline_mode=pl.Buffered(1))


_ROPE_PERM = np.concatenate([np.arange(8, 16), np.arange(0, 8), np.arange(24, 32), np.arange(16, 24)])
_ROPE_SIGN = np.concatenate([-np.ones(8), np.ones(8), -np.ones(8), np.ones(8)]).astype(np.float32)


@functools.lru_cache(maxsize=None)
def _rope_tables():
    t = np.arange(SEQ)
    row = (t // GRID_W).astype(np.float32)
    col = (t % GRID_W).astype(np.float32)
    half = MLA_ROPE // 2
    inv = np.power(np.float32(ROPE_BASE), -np.arange(0, half, 2, dtype=np.float32) / np.float32(half))
    ar, ac = row[:, None] * inv, col[:, None] * inv
    ang = np.concatenate([ar, ar, ac, ac], axis=-1).astype(np.float32)
    cos = np.ones((ROWS, MLA_ROPE), np.float32)
    sin = np.zeros((ROWS, MLA_ROPE), np.float32)
    cos[:SEQ] = np.cos(ang)
    sin[:SEQ] = np.sin(ang)
    scale = np.float32((MLA_NOPE + MLA_ROPE) ** -0.5)
    cos_q = np.zeros((ROWS, MLA_SLOT), np.float32)
    sin_q = np.zeros((ROWS, MLA_SLOT), np.float32)
    cos_q[:, :MLA_NOPE] = scale
    cos_q[:, MLA_NOPE:MLA_NOPE + MLA_ROPE] = cos * scale
    sin_q[:, MLA_NOPE:MLA_NOPE + MLA_ROPE] = sin * scale
    cos_k = np.zeros((ROWS, MLA_SLOT), np.float32)
    sin_k = np.zeros((ROWS, MLA_SLOT), np.float32)
    cos_k[:, MLA_NOPE:MLA_NOPE + MLA_ROPE] = cos
    sin_k[:, MLA_NOPE:MLA_NOPE + MLA_ROPE] = sin
    return cos_q, sin_q, cos_k, sin_k


@functools.lru_cache(maxsize=None)
def _dft_tables(L):
    n = 2 * L
    k = np.arange(L, dtype=np.int64)
    prod = np.mod(np.outer(2 * k + 1, 2 * k + 1), 4 * n)
    ang = prod.astype(np.float64) * (math.pi / (2 * n))
    c2 = np.cos(ang).astype(np.float32)
    s2 = np.sin(ang).astype(np.float32)
    phi = (k.astype(np.float64) + 0.5) * (math.pi / n)
    cphi = np.cos(phi).astype(np.float32)[:, None]
    sphi = np.sin(phi).astype(np.float32)[:, None]
    return c2, s2, cphi, sphi


@functools.lru_cache(maxsize=None)
def _hyena_pos_features(L):
    f32 = np.float32
    t = np.arange(L, dtype=f32)
    t_norm = t / f32(max(L - 1, 1))
    bands = (HY_EMB - 1) // 2
    fr = np.linspace(1e-4, bands - 1, bands, dtype=f32)
    ang = (f32(2.0 * math.pi / L) * t[:, None] * fr[None, :]).astype(f32)
    z = np.concatenate([t_norm[:, None], np.cos(ang), -np.sin(ang)], axis=-1).astype(f32)
    zp = np.zeros((L, 128), f32)
    zp[:, :HY_EMB] = z
    max_decay = math.log(HY_TARGET) / HY_FAST_DECAY_PCT
    min_decay = math.log(HY_TARGET) / HY_SLOW_DECAY_PCT
    deltas = np.abs(np.linspace(min_decay, max_decay, HY_WIDTH, dtype=f32)).astype(f32)
    return zp, t_norm[:, None].astype(f32), deltas[None, :]


@functools.lru_cache(maxsize=None)
def _na_bias_index():
    wr = NA_WIN_R
    j = np.arange(GRID_W)
    c0 = np.clip(j - NA_WIN_C // 2, 0, GRID_W - NA_WIN_C)
    col_mask = (j[None, :] >= c0[:, None]) & (j[None, :] < c0[:, None] + NA_WIN_C)
    dcol = np.clip(j[None, :] - j[:, None], -(NA_WIN_C - 1), NA_WIN_C - 1) + NA_WIN_C - 1
    nq, nk = NA_GROUP * GRID_W, NA_KROWS * GRID_W
    drow_i = np.zeros((3, nq, nk), np.int32)
    dcol_i = np.zeros((3, nq, nk), np.int32)
    valid = np.zeros((3, nq, nk), bool)
    for p, ra in enumerate((0, NA_GROUP, GRID_ROWS - NA_GROUP)):
        ks = int(np.clip(ra - wr // 2, 0, GRID_ROWS - NA_KROWS))
        for i in range(NA_GROUP):
            r = ra + i
            r0 = int(np.clip(r - wr // 2, 0, GRID_ROWS - wr))
            for jj in range(NA_KROWS):
                kr = ks + jj
                ok = (r0 <= kr) and (kr < r0 + wr)
                qs = slice(i * GRID_W, (i + 1) * GRID_W)
                ksl = slice(jj * GRID_W, (jj + 1) * GRID_W)
                drow_i[p, qs, ksl] = int(np.clip(kr - r + wr - 1, 0, 2 * wr - 2))
                dcol_i[p, qs, ksl] = dcol
                valid[p, qs, ksl] = col_mask & ok
    return drow_i, dcol_i, valid


def _adaln_kernel(c_ref, w_ref, b_ref, o_ref):
    c = c_ref[...]
    a = c * (1.0 / (1.0 + jnp.exp(-c)))
    o_ref[...] = jnp.dot(a, w_ref[...], preferred_element_type=F32,
                         precision=lax.Precision.HIGHEST) + b_ref[...]


def _adaln(cc, w, b):
    rows = cc.shape[0]
    tn = 768
    return pl.pallas_call(
        _adaln_kernel,
        grid=(3 * D_MODEL // tn,),
        in_specs=[pl.BlockSpec((rows, D_MODEL), lambda n: (0, 0)),
                  pl.BlockSpec((D_MODEL, tn), lambda n: (0, n)),
                  pl.BlockSpec((1, tn), lambda n: (0, n))],
        out_specs=pl.BlockSpec((rows, tn), lambda n: (0, n)),
        out_shape=jax.ShapeDtypeStruct((rows, 3 * D_MODEL), F32),
        compiler_params=_params(("arbitrary",)),
        name="adaln",
    )(cc, w, b.reshape(1, -1))


def _inproj_kernel(x_ref, cx_ref, modl_ref, modc_ref, g_ref, w_ref, o_ref, h_ref):
    @pl.when(pl.program_id(1) == 0)
    def _():
        g = g_ref[...]

        def norm_mod(xv, mod):
            ms = jnp.mean(xv * xv, axis=-1, keepdims=True)
            xn = xv * lax.rsqrt(ms + EPS) * g
            return (xn * (1.0 + mod[:, D_MODEL:2 * D_MODEL]) + mod[:, 0:D_MODEL]).astype(BF16)

        modl = modl_ref[0]
        for i in range(N_LAT_TILES):
            h_ref[i * TILE:(i + 1) * TILE, :] = norm_mod(x_ref[0, i * TILE:(i + 1) * TILE, :], modl)
        h_ref[SEQ:ROWS, :] = norm_mod(cx_ref[0], modc_ref[0])

    o_ref[0] = jnp.dot(h_ref[...], w_ref[...], preferred_element_type=F32).astype(BF16)


def _inproj(x, cx, mod3, pre_g, w_big):
    B = x.shape[0]
    return pl.pallas_call(
        _inproj_kernel,
        grid=(B, P_COLS // INPROJ_TN),
        in_specs=[pl.BlockSpec((1, SEQ, D_MODEL), lambda b, n: (b, 0, 0)),
                  pl.BlockSpec((1, CTX_LEN, D_MODEL), lambda b, n: (b, 0, 0)),
                  pl.BlockSpec((1, 1, 3 * D_MODEL), lambda b, n: (b, 0, 0)),
                  pl.BlockSpec((1, 1, 3 * D_MODEL), lambda b, n: (B, 0, 0)),
                  pl.BlockSpec((1, D_MODEL), lambda b, n: (0, 0)),
                  pl.BlockSpec((D_MODEL, INPROJ_TN), lambda b, n: (0, n))],
        out_specs=pl.BlockSpec((1, ROWS, INPROJ_TN), lambda b, n: (b, 0, n)),
        out_shape=jax.ShapeDtypeStruct((B, ROWS, P_COLS), BF16),
        scratch_shapes=[pltpu.VMEM((ROWS, D_MODEL), BF16)],
        compiler_params=_params(("arbitrary", "arbitrary")),
        name="inproj",
    )(x, cx, mod3, mod3, pre_g.reshape(1, -1), w_big)


def _mla_prep_kernel(p_ref, qn_ref, kvn_ref, wq_ref, wqx_ref, wk_ref, wv_ref, e1_ref, e2_ref,
                     cq_ref, sq_ref, ck_ref, sk_ref, q_ref, k_ref, v_ref):
    pm = p_ref[0]
    cq = pm[:, 0:MLA_Q_RANK].astype(F32)
    ckv = pm[:, MLA_Q_RANK:MLA_Q_RANK + MLA_KV_RANK].astype(F32)
    misc = pm[:, MLA_Q_RANK + MLA_KV_RANK:]
    cqn = (cq * lax.rsqrt(jnp.mean(cq * cq, axis=-1, keepdims=True) + EPS) * qn_ref[...]).astype(BF16)
    ckvn = (ckv * lax.rsqrt(jnp.mean(ckv * ckv, axis=-1, keepdims=True) + EPS) * kvn_ref[...]).astype(BF16)
    q = jnp.dot(cqn, wq_ref[...], preferred_element_type=F32)
    qx = jnp.dot(cqn, wqx_ref[...], preferred_element_type=F32)
    kn = jnp.dot(ckvn, wk_ref[...], preferred_element_type=F32)
    v = jnp.dot(ckvn, wv_ref[...], preferred_element_type=F32)
    kr = jnp.dot(misc, e1_ref[...], preferred_element_type=F32)
    krx = jnp.dot(misc, e2_ref[...], preferred_element_type=F32)
    krot = kr * ck_ref[...] + krx * sk_ref[...]
    cosq, sinq = cq_ref[...], sq_ref[...]
    for h in range(MLA_HEADS):
        sl = slice(h * MLA_SLOT, (h + 1) * MLA_SLOT)
        q_ref[0, :, sl] = (q[:, sl] * cosq + qx[:, sl] * sinq).astype(BF16)
        k_ref[0, :, sl] = (kn[:, sl] + krot).astype(BF16)
    v_ref[0] = v.astype(BF16)


def _mla_prep(P, wts, n_tiles):
    B = P.shape[0]
    width = MLA_HEADS * MLA_SLOT
    cos_q, sin_q, cos_k, sin_k = (jnp.asarray(t) for t in _rope_tables())
    const = lambda b, t: (0, 0)
    tab = pl.BlockSpec((TILE, MLA_SLOT), lambda b, t: (t, 0))
    out = pl.BlockSpec((1, TILE, width), lambda b, t: (b, t, 0))
    shp = jax.ShapeDtypeStruct((B, ROWS, width), BF16)
    return pl.pallas_call(
        _mla_prep_kernel,
        grid=(B, N_TILES),
        in_specs=[pl.BlockSpec((1, TILE, 512), lambda b, t: (b, t, OFF_MLA // 512)),
                  pl.BlockSpec((1, MLA_Q_RANK), const),
                  pl.BlockSpec((1, MLA_KV_RANK), const),
                  pl.BlockSpec((MLA_Q_RANK, width), const),
                  pl.BlockSpec((MLA_Q_RANK, width), const),
                  pl.BlockSpec((MLA_KV_RANK, width), const),
                  pl.BlockSpec((MLA_KV_RANK, width), const),
                  pl.BlockSpec((128, MLA_SLOT), const),
                  pl.BlockSpec((128, MLA_SLOT), const),
                  tab, tab, tab, tab],
        out_specs=[out, out, out],
        out_shape=[shp, shp, shp],
        compiler_params=_params(("arbitrary", "arbitrary")),
        name="mla_prep",
    )(P, wts['q_norm'], wts['kv_norm'], wts['wq'], wts['wqx'], wts['wk'], wts['wv'],
      wts['e1'], wts['e2'], cos_q, sin_q, cos_k, sin_k)


def _mla_attn_kernel(q_ref, k_ref, v_ref, o_ref):
    def attend(k0, nk):
        acc = None
        for j in range(2):
            sl = slice(j * MLA_SLOT, (j + 1) * MLA_SLOT)
            q = q_ref[0, :, sl]
            k = k_ref[0, k0:k0 + nk, sl]
            v = v_ref[0, k0:k0 + nk, sl]
            s = lax.dot_general(q, k, (((1,), (1,)), ((), ())), preferred_element_type=F32)
            m = jnp.max(s, axis=-1, keepdims=True)
            p = jnp.exp(s - m)
            l = jnp.sum(p, axis=-1, keepdims=True)
            o = jnp.dot(p.astype(BF16), v, preferred_element_type=F32) * (1.0 / l)
            acc = o if acc is None else acc + o
        o_ref[0] = acc.astype(BF16)

    is_ctx = pl.program_id(2) == N_LAT_TILES

    @pl.when(jnp.logical_not(is_ctx))
    def _():
        attend(0, ROWS)

    @pl.when(is_ctx)
    def _():
        attend(SEQ, CTX_LEN)


def _mla_attn(q, k, v, n_tiles):
    B = q.shape[0]
    pair = 2 * MLA_SLOT
    return pl.pallas_call(
        _mla_attn_kernel,
        grid=(B, MLA_HEADS // 2, n_tiles),
        in_specs=[pl.BlockSpec((1, TILE, pair), lambda b, p, t: (b, t, p)),
                  pl.BlockSpec((1, ROWS, pair), lambda b, p, t: (b, 0, p)),
                  pl.BlockSpec((1, ROWS, pair), lambda b, p, t: (b, 0, p))],
        out_specs=pl.BlockSpec((1, TILE, MLA_SLOT), lambda b, p, t: (b, t, p)),
        out_shape=jax.ShapeDtypeStruct((B, ROWS, MLA_HEADS * MLA_V), BF16),
        compiler_params=_params(("arbitrary", "arbitrary", "arbitrary")),
        name="mla_attn",
    )(q, k, v)


def _na_kernel(q_ref, k_ref, v_ref, bias_ref, o_ref):
    g = pl.program_id(1)
    lane = lax.broadcasted_iota(jnp.int32, (1, 2 * NA_HEAD_DIM), 1)
    scale = NA_HEAD_DIM ** -0.5
    nk_win = NA_KROWS * GRID_W

    def head_masks(h):
        return (lane >= NA_HEAD_DIM) if (h % 2) else (lane < NA_HEAD_DIM)

    def finish(parts, l):
        inv = 1.0 / l
        return sum(parts[1:], parts[0]) * inv

    @pl.when(g < N_LAT_TILES)
    def _():
        ks = jnp.clip(g * NA_GROUP - NA_WIN_R // 2, 0, GRID_ROWS - NA_KROWS)
        start = pl.multiple_of(ks * GRID_W, GRID_W)
        for p in range(NA_HEADS // 2):
            sl = slice(p * 128, (p + 1) * 128)
            qp = q_ref[0, :, sl] * scale
            kw = k_ref[0, pl.ds(start, nk_win), sl]
            vw = v_ref[0, pl.ds(start, nk_win), sl]
            kc = k_ref[0, SEQ:ROWS, sl]
            vc = v_ref[0, SEQ:ROWS, sl]
            acc = None
            for h in (2 * p, 2 * p + 1):
                mk = head_masks(h)
                qm = jnp.where(mk, qp, 0).astype(BF16)
                s_w = lax.dot_general(qm, kw, (((1,), (1,)), ((), ())),
                                      preferred_element_type=F32) + bias_ref[0, h]
                s_c = lax.dot_general(qm, kc, (((1,), (1,)), ((), ())), preferred_element_type=F32)
                m = jnp.maximum(jnp.max(s_w, axis=-1, keepdims=True), jnp.max(s_c, axis=-1, keepdims=True))
                p_w = jnp.exp(s_w - m)
                p_c = jnp.exp(s_c - m)
                l = jnp.sum(p_w, axis=-1, keepdims=True) + jnp.sum(p_c, axis=-1, keepdims=True)
                o = finish([jnp.dot(p_w.astype(BF16), jnp.where(mk, vw, 0).astype(BF16), preferred_element_type=F32),
                            jnp.dot(p_c.astype(BF16), jnp.where(mk, vc, 0).astype(BF16), preferred_element_type=F32)], l)
                acc = o if acc is None else acc + o
            o_ref[0, :, sl] = acc.astype(BF16)

    @pl.when(g == N_LAT_TILES)
    def _():
        for p in range(NA_HEADS // 2):
            sl = slice(p * 128, (p + 1) * 128)
            qp = q_ref[0, :, sl] * scale
            kc = k_ref[0, SEQ:ROWS, sl]
            vc = v_ref[0, SEQ:ROWS, sl]
            acc = None
            for h in (2 * p, 2 * p + 1):
                mk = head_masks(h)
                qm = jnp.where(mk, qp, 0).astype(BF16)
                s_c = lax.dot_general(qm, kc, (((1,), (1,)), ((), ())), preferred_element_type=F32)
                m = jnp.max(s_c, axis=-1, keepdims=True)
                p_c = jnp.exp(s_c - m)
                l = jnp.sum(p_c, axis=-1, keepdims=True)
                o = finish([jnp.dot(p_c.astype(BF16), jnp.where(mk, vc, 0).astype(BF16), preferred_element_type=F32)], l)
                acc = o if acc is None else acc + o
            o_ref[0, :, sl] = acc.astype(BF16)


def _na_attn(P, bias_tab, n_tiles):
    B = P.shape[0]
    width = NA_HEADS * NA_HEAD_DIM
    nq, nk = NA_GROUP * GRID_W, NA_KROWS * GRID_W

    def bias_map(b, g):
        return (jnp.where(g == 0, 0, jnp.where(g == N_LAT_TILES - 1, 2, 1)), 0, 0, 0)

    return pl.pallas_call(
        _na_kernel,
        grid=(B, n_tiles),
        in_specs=[pl.BlockSpec((1, TILE, width), lambda b, g: (b, g, OFF_NA_Q // width)),
                  pl.BlockSpec((1, ROWS, width), lambda b, g: (b, 0, OFF_NA_K // width)),
                  pl.BlockSpec((1, ROWS, width), lambda b, g: (b, 0, OFF_NA_V // width)),
                  pl.BlockSpec((1, NA_HEADS, nq, nk), bias_map)],
        out_specs=pl.BlockSpec((1, TILE, width), lambda b, g: (b, g, 0)),
        out_shape=jax.ShapeDtypeStruct((B, ROWS, width), BF16),
        compiler_params=_params(("arbitrary", "arbitrary")),
        name="na_attn",
    )(P, P, P, bias_tab)


def _na_bias_table(rpb):
    drow, dcol, valid = _na_bias_index()
    tab = rpb.astype(F32)[:, drow, dcol]
    tab = jnp.where(valid[None], tab, NA_NEG)
    return tab.transpose(1, 0, 2, 3)


def _hy_filter_kernel(z_ref, tn_ref, dl_ref, w1_ref, b1_ref, fr_ref, w2_ref, b2_ref, w3_ref, o_ref):
    hp = lax.Precision.HIGHEST
    fr = fr_ref[...]
    a = jnp.sin(fr[0:1, :] * (jnp.dot(z_ref[...], w1_ref[...], preferred_element_type=F32, precision=hp)
                              + b1_ref[...]))
    a = jnp.sin(fr[1:2, :] * (jnp.dot(a, w2_ref[...], preferred_element_type=F32, precision=hp)
                              + b2_ref[...]))
    filt = jnp.dot(a, w3_ref[...], preferred_element_type=F32, precision=hp)
    window = jnp.exp(-tn_ref[...] * dl_ref[...])
    f0 = filt[:, 0:HY_CT] * window
    f1 = filt[:, HY_CT:] * window
    den = (jnp.sum(jnp.abs(f0), axis=0, keepdims=True) + jnp.sum(jnp.abs(f1), axis=0, keepdims=True)) + EPS
    o_ref[:, 0:HY_CT] = f0 / den
    o_ref[:, HY_CT:] = f1 / den


def _hy_filters(L, w1p, b1, freq, w2, b2, w3):
    zp, t_norm, deltas = (jnp.asarray(t) for t in _hyena_pos_features(L))
    n_ct = HY_WIDTH // HY_CT
    w3t = w3.reshape(HY_FFN, 2, 2, n_ct, HY_CT).transpose(0, 1, 3, 2, 4).reshape(HY_FFN, 4 * HY_WIDTH)
    const = lambda n: (0, 0)
    return pl.pallas_call(
        _hy_filter_kernel,
        grid=(2 * n_ct,),
        in_specs=[pl.BlockSpec((L, 128), const), pl.BlockSpec((L, 1), const),
                  pl.BlockSpec((1, HY_CT), lambda n: (0, n % n_ct)), pl.BlockSpec((128, HY_FFN), const),
                  pl.BlockSpec((1, HY_FFN), const), pl.BlockSpec((2, HY_FFN), const),
                  pl.BlockSpec((HY_FFN, HY_FFN), const), pl.BlockSpec((1, HY_FFN), const),
                  pl.BlockSpec((HY_FFN, 2 * HY_CT), lambda n: (0, n))],
        out_specs=pl.BlockSpec((L, 2 * HY_CT), lambda n: (0, n)),
        out_shape=jax.ShapeDtypeStruct((L, 4 * HY_WIDTH), F32),
        compiler_params=_params(("arbitrary",)),
        name="hyena_filters",
    )(zp, t_norm, deltas, w1p, b1.reshape(1, -1), freq, w2, b2.reshape(1, -1), w3t)


def _hy_spectrum_kernel(f_ref, c2_ref, s2_ref, cp_ref, sp_ref, o_ref, *, L):
    half = f_ref.shape[1] // 2
    row = lax.broadcasted_iota(jnp.int32, (L, half), 0)
    ff = f_ref[:, 0:half]
    fb = jnp.where(row == 0, 0.0, f_ref[:, half:])

    def split(x):
        hi = x.astype(BF16)
        return hi, (x - hi.astype(F32)).astype(BF16)

    def xform(m_ref, parts):
        return (jnp.dot(m_ref[...], parts[0], preferred_element_type=F32)
                + jnp.dot(m_ref[...], parts[1], preferred_element_type=F32))

    pf, pb = split(ff), split(fb)
    a_f, b_f = xform(c2_ref, pf), xform(s2_ref, pf)
    a_b, b_b = xform(c2_ref, pb), xform(s2_ref, pb)
    cp, sp = cp_ref[...], sp_ref[...]
    scale = 1.0 / L
    o_ref[0] = (cp * (a_f + a_b) + sp * (b_f + b_b)) * scale
    o_ref[1] = (sp * (a_f - a_b) - cp * (b_f - b_b)) * scale


def _hy_spectrum(filt, L):
    c2, s2, cphi, sphi = _dft_tables(L)
    c2, s2 = jnp.asarray(c2, BF16), jnp.asarray(s2, BF16)
    n_ct = HY_WIDTH // HY_CT
    ft = min(L, 512)
    out = pl.pallas_call(
        functools.partial(_hy_spectrum_kernel, L=L),
        grid=(2 * n_ct, L // ft),
        in_specs=[pl.BlockSpec((L, 2 * HY_CT), lambda n, f: (0, n)),
                  pl.BlockSpec((ft, L), lambda n, f: (f, 0)),
                  pl.BlockSpec((ft, L), lambda n, f: (f, 0)),
                  pl.BlockSpec((ft, 1), lambda n, f: (f, 0)),
                  pl.BlockSpec((ft, 1), lambda n, f: (f, 0))],
        out_specs=pl.BlockSpec((2, ft, HY_CT), lambda n, f: (0, f, n)),
        out_shape=jax.ShapeDtypeStruct((2, L, 2 * n_ct * HY_CT), F32),
        compiler_params=_params(("arbitrary", "arbitrary")),
        name="hyena_spectrum",
    )(filt, c2, s2, jnp.asarray(cphi), jnp.asarray(sphi))
    return out, c2, s2


def _hy_conv_kernel(x1_ref, x2_ref, z_ref, w_ref, b_ref, skip_ref, h_ref, c2_ref, s2_ref, o_ref,
                    yre_ref, yim_ref, *, L):
    row = lax.broadcasted_iota(jnp.int32, (L, HY_CT), 0)

    def short_conv(p_ref, i):
        p = p_ref[0].astype(F32)
        w = w_ref[i]
        prev = jnp.where(row == 0, 0.0, pltpu.roll(p, 1, 0))
        nxt = jnp.where(row == L - 1, 0.0, pltpu.roll(p, L - 1, 0))
        return prev * w[0:1, :] + p * w[1:2, :] + nxt * w[2:3, :] + b_ref[i]

    ft = min(L, 512)
    z = short_conv(z_ref, 2)
    for n, gate_ref in enumerate((x1_ref, x2_ref)):
        zb = z.astype(BF16)
        for f in range(L // ft):
            fs = slice(f * ft, (f + 1) * ft)
            ur = jnp.dot(c2_ref[fs, :], zb, preferred_element_type=F32)
            us = jnp.dot(s2_ref[fs, :], zb, preferred_element_type=F32)
            hre, him = h_ref[0, n, fs, :], h_ref[1, n, fs, :]
            yre_ref[fs, :] = (ur * hre + us * him).astype(BF16)
            yim_ref[fs, :] = (ur * him - us * hre).astype(BF16)
        y = (jnp.dot(c2_ref[...], yre_ref[...], preferred_element_type=F32)
             - jnp.dot(s2_ref[...], yim_ref[...], preferred_element_type=F32))
        z = short_conv(gate_ref, n) * (y + z * skip_ref[n])
    o_ref[0] = z.astype(BF16)


def _hy_conv(P, row_blk, L, conv_w, conv_b, skip, spec, c2, s2):
    B = P.shape[0]
    n_ct = HY_WIDTH // HY_CT
    w4 = conv_w.reshape(3, 3, n_ct, HY_CT).transpose(2, 1, 0, 3)
    b4 = conv_b.reshape(3, n_ct, 1, HY_CT).transpose(1, 0, 2, 3)
    sk4 = skip.reshape(2, n_ct, 1, HY_CT).transpose(1, 0, 2, 3)
    h5 = spec.reshape(2, L, 2, n_ct, HY_CT).transpose(3, 0, 2, 1, 4)

    def pcol(off):
        return pl.BlockSpec((1, L, HY_CT), lambda c, b: (b, row_blk, off // HY_CT + c))

    const = lambda c, b: (0, 0)
    return pl.pallas_call(
        functools.partial(_hy_conv_kernel, L=L),
        grid=(n_ct, B),
        in_specs=[pcol(OFF_HY_X1), pcol(OFF_HY_X2), pcol(OFF_HY_ZIN),
                  pl.BlockSpec((None, 3, 3, HY_CT), lambda c, b: (c, 0, 0, 0)),
                  pl.BlockSpec((None, 3, 1, HY_CT), lambda c, b: (c, 0, 0, 0)),
                  pl.BlockSpec((None, 2, 1, HY_CT), lambda c, b: (c, 0, 0, 0)),
                  pl.BlockSpec((None, 2, 2, L, HY_CT), lambda c, b: (c, 0, 0, 0, 0),
                               pipeline_mode=pl.Buffered(1)),
                  _single((L, L), const), _single((L, L), const)],
        out_specs=pl.BlockSpec((1, L, HY_CT), lambda c, b: (b, 0, c)),
        out_shape=jax.ShapeDtypeStruct((B, L, HY_WIDTH), BF16),
        scratch_shapes=[pltpu.VMEM((L, HY_CT), BF16), pltpu.VMEM((L, HY_CT), BF16)],
        compiler_params=_params(("arbitrary", "arbitrary")),
        name="hyena_conv",
    )(P, P, P, w4, b4, sk4, h5, c2, s2)


def _gla_kernel(qk_ref, v_ref, misc_ref, wg_ref, bg_ref, ng_ref, o_ref, g_scr, o_scr, s_scr):
    C = GLA_CHUNK
    HK = GLA_HEADS * GLA_DK
    logit = jnp.dot(misc_ref[0], wg_ref[...], preferred_element_type=F32) + bg_ref[...]
    g_scr[...] = (jnp.minimum(logit, 0.0) - jnp.log(1.0 + jnp.exp(-jnp.abs(logit)))) * (1.0 / GLA_TAU)
    s_scr[...] = jnp.zeros_like(s_scr)

    ri = lax.broadcasted_iota(jnp.int32, (C, C), 0)
    ci = lax.broadcasted_iota(jnp.int32, (C, C), 1)
    lane = lax.broadcasted_iota(jnp.int32, (1, 2 * GLA_DK), 1)
    lower = ri >= ci
    tri = (lower.astype(F32), jnp.logical_not(ri > ci).astype(F32))
    keep = (lower, ri <= ci)
    last = (C - 1, 0)
    mid = (C // 2 - 1, C // 2)

    def one_chunk(d, c):
        r0 = pl.multiple_of(c * C, C)
        rows = pl.ds(r0, C)
        g = g_scr[rows, d * HK:(d + 1) * HK]
        bc = jnp.dot(tri[d], g, preferred_element_type=F32, precision=lax.Precision.HIGHEST)
        bl = bc[last[d]:last[d] + 1, :]
        bm = bc[mid[d]:mid[d] + 1, :]
        q = qk_ref[0, rows, 0:HK].astype(F32) * (GLA_DK ** -0.5)
        k = qk_ref[0, rows, HK:2 * HK].astype(F32)
        qe = (q * jnp.exp(bc - bm)).astype(BF16)
        ke = (k * jnp.exp(bm - bc)).astype(BF16)
        qi = (q * jnp.exp(bc)).astype(BF16)
        ko = (k * jnp.exp(bl - bc)).astype(BF16)
        dec = jnp.exp(bl)
        for h in range(GLA_HEADS):
            p = h // 2
            sl = slice(p * 128, (p + 1) * 128)
            mk = (lane >= GLA_DK) if (h % 2) else (lane < GLA_DK)
            v_h = v_ref[0, rows, h * GLA_DV:(h + 1) * GLA_DV]
            qe_h = jnp.where(mk, qe[:, sl], 0).astype(BF16)
            a = lax.dot_general(qe_h, ke[:, sl], (((1,), (1,)), ((), ())), preferred_element_type=F32)
            a = jnp.where(keep[d], a, 0.0).astype(BF16)
            st = s_scr[d, h]
            qi_h = jnp.where(mk, qi[:, sl], 0).astype(BF16)
            o = (jnp.dot(a, v_h, preferred_element_type=F32)
                 + lax.dot_general(qi_h, st.astype(BF16), (((1,), (1,)), ((), ())),
                                   preferred_element_type=F32))
            ko_h = jnp.where(mk, ko[:, sl], 0).astype(BF16)
            kv_t = lax.dot_general(v_h, ko_h, (((0,), (0,)), ((), ())), preferred_element_type=F32)
            s_scr[d, h] = st * dec[:, sl] + kv_t
            o_scr[d, rows, h * GLA_DV:(h + 1) * GLA_DV] = o

    def step(i, carry):
        cf = jnp.where(i < GLA_CTX_CHUNKS, GLA_NCHUNK - GLA_CTX_CHUNKS + i, i - GLA_CTX_CHUNKS)
        cb = GLA_NCHUNK - 1 - i
        one_chunk(0, cf)
        one_chunk(1, cb)
        return carry

    lax.fori_loop(0, GLA_NCHUNK, step, 0)

    ng = ng_ref[...]
    for t in range(N_TILES):
        rs = slice(t * TILE, (t + 1) * TILE)
        for h in range(GLA_HEADS):
            cs = slice(h * GLA_DV, (h + 1) * GLA_DV)
            o = o_scr[0, rs, cs] + o_scr[1, rs, cs]
            o = o * lax.rsqrt(jnp.mean(o * o, axis=-1, keepdims=True) + EPS) * ng
            o_ref[0, rs, cs] = o.astype(BF16)


def _gla(P, wg, bg, norm_g):
    B = P.shape[0]
    width = GLA_HEADS * GLA_DV
    const = lambda b: (0, 0)
    return pl.pallas_call(
        _gla_kernel,
        grid=(B,),
        in_specs=[pl.BlockSpec((1, ROWS, 512), lambda b: (b, 0, OFF_GLA_QK // 512)),
                  pl.BlockSpec((1, ROWS, width), lambda b: (b, 0, OFF_GLA_V // width)),
                  pl.BlockSpec((1, ROWS, 128), lambda b: (b, 0, (OFF_MLA + 384) // 128)),
                  pl.BlockSpec((128, 2 * GLA_HEADS * GLA_DK), const),
                  pl.BlockSpec((1, 2 * GLA_HEADS * GLA_DK), const),
                  pl.BlockSpec((1, GLA_DV), const)],
        out_specs=pl.BlockSpec((1, ROWS, width), lambda b: (b, 0, 0)),
        out_shape=jax.ShapeDtypeStruct((B, ROWS, width), BF16),
        scratch_shapes=[pltpu.VMEM((ROWS, 2 * GLA_HEADS * GLA_DK), F32),
                        pltpu.VMEM((2, ROWS, width), F32),
                        pltpu.VMEM((2, GLA_HEADS, GLA_DV, 2 * GLA_DK), F32)],
        compiler_params=_params(("arbitrary",)),
        name="gla",
    )(P, P, P, wg, bg, norm_g.reshape(1, -1))


def _merge_kernel(ya_ref, yb_ref, yh_ref, yg_ref, za_ref, zb_ref, zh_ref, zg_ref, gt_ref,
                  x_ref, mod_ref, pg_ref, wb_ref, wo_ref, o_ref):
    merged = None
    for i, (y_ref, z_ref) in enumerate(((ya_ref, za_ref), (yb_ref, zb_ref), (yh_ref, zh_ref), (yg_ref, zg_ref))):
        z = z_ref[0].astype(F32)
        t = (y_ref[0].astype(F32) * (z * (1.0 / (1.0 + jnp.exp(-z))))).astype(BF16)
        br = jnp.dot(t, wb_ref[i], preferred_element_type=F32)
        gl = gt_ref[0, :, i * D_MODEL:(i + 1) * D_MODEL].astype(F32)
        term = br * (1.0 / (1.0 + jnp.exp(-gl)))
        merged = term if merged is None else merged + term
    out = jnp.dot(merged.astype(BF16), wo_ref[...], preferred_element_type=F32)
    xn = out * lax.rsqrt(jnp.mean(out * out, axis=-1, keepdims=True) + EPS) * pg_ref[...]
    gate = mod_ref[0][:, 2 * D_MODEL:3 * D_MODEL]
    o_ref[0] = x_ref[0] + gate * xn


def _merge(x, P, ya, yb, yh, yg, mod3, mod_row, post_g, wb, wo, row_off):
    B, R, _ = x.shape
    n_t = R // TILE
    const2 = lambda b, t: (0, 0)

    def uni(width, off):
        return pl.BlockSpec((1, TILE, width), lambda b, t: (b, t + row_off, off // width))

    return pl.pallas_call(
        _merge_kernel,
        grid=(B, n_t),
        in_specs=[uni(512, 0), uni(512, 0),
                  pl.BlockSpec((1, TILE, 512), lambda b, t: (b, t, 0)),
                  uni(512, 0),
                  uni(512, OFF_MLA_Z), uni(512, OFF_NA_Z), uni(512, OFF_HY_Z), uni(512, OFF_GLA_Z),
                  uni(4 * D_MODEL, OFF_MERGE),
                  pl.BlockSpec((1, TILE, D_MODEL), lambda b, t: (b, t, 0)),
                  pl.BlockSpec((1, 1, 3 * D_MODEL), lambda b, t: (mod_row(b), 0, 0)),
                  pl.BlockSpec((1, D_MODEL), const2),
                  pl.BlockSpec((4, 512, D_MODEL), lambda b, t: (0, 0, 0)),
                  pl.BlockSpec((D_MODEL, D_MODEL), const2)],
        out_specs=pl.BlockSpec((1, TILE, D_MODEL), lambda b, t: (b, t, 0)),
        out_shape=jax.ShapeDtypeStruct((B, R, D_MODEL), F32),
        compiler_params=_params(("arbitrary", "arbitrary")),
        name="merge",
    )(ya, yb, yh, yg, P, P, P, P, P, x, mod3, post_g.reshape(1, -1), wb, wo)


def _build_w_big(w):
    def cols(name):
        a, b = _ref_cols(name)
        return w[:, a:b]

    kr = cols('mla_kr')
    krx = kr[:, _ROPE_PERM] * _ROPE_SIGN
    pad32 = jnp.zeros((D_MODEL, 32), w.dtype)
    parts = [cols('merge'), cols('mla_cq'), cols('mla_ckv'), kr, cols('gla_glr'), krx, pad32,
             cols('mla_z'), cols('na_qkv'), cols('na_z'), cols('hy_proj'), cols('hy_z'),
             cols('gla_qk'), cols('gla_v'), cols('gla_z')]
    big = jnp.concatenate(parts, axis=1).astype(BF16)
    assert big.shape == (D_MODEL, P_COLS)
    return big


def _build_mla_weights(q_norm, w_uq, kv_norm, w_ukv):
    dq = MLA_NOPE + MLA_ROPE
    wq3 = w_uq.reshape(MLA_Q_RANK, MLA_HEADS, dq)
    rot = wq3[:, :, MLA_NOPE:]
    rotx = rot[:, :, _ROPE_PERM] * _ROPE_SIGN
    zq = jnp.zeros((MLA_Q_RANK, MLA_HEADS, MLA_SLOT - dq), F32)
    wq = jnp.concatenate([wq3, zq], axis=-1).reshape(MLA_Q_RANK, MLA_HEADS * MLA_SLOT)
    wqx = jnp.concatenate([jnp.zeros((MLA_Q_RANK, MLA_HEADS, MLA_NOPE), F32), rotx, zq],
                          axis=-1).reshape(MLA_Q_RANK, MLA_HEADS * MLA_SLOT)
    wkv3 = w_ukv.reshape(MLA_KV_RANK, MLA_HEADS, MLA_NOPE + MLA_V)
    zk = jnp.zeros((MLA_KV_RANK, MLA_HEADS, MLA_SLOT - MLA_NOPE), F32)
    wk = jnp.concatenate([wkv3[:, :, :MLA_NOPE], zk], axis=-1).reshape(MLA_KV_RANK, MLA_HEADS * MLA_SLOT)
    wv3 = wkv3[:, :, MLA_NOPE:]
    zv = jnp.zeros_like(wv3)
    even = jnp.concatenate([wv3, zv], axis=-1)
    odd = jnp.concatenate([zv, wv3], axis=-1)
    sel = (np.arange(MLA_HEADS) % 2 == 1)[None, :, None]
    wv = jnp.where(sel, odd, even).reshape(MLA_KV_RANK, MLA_HEADS * MLA_SLOT)
    e1 = np.zeros((128, MLA_SLOT), np.float32)
    e2 = np.zeros((128, MLA_SLOT), np.float32)
    for i in range(MLA_ROPE):
        e1[i, MLA_NOPE + i] = 1.0
        e2[64 + i, MLA_NOPE + i] = 1.0
    return dict(q_norm=q_norm.reshape(1, -1), kv_norm=kv_norm.reshape(1, -1),
                wq=wq.astype(BF16), wqx=wqx.astype(BF16), wk=wk.astype(BF16), wv=wv.astype(BF16),
                e1=jnp.asarray(e1, BF16), e2=jnp.asarray(e2, BF16))


def _build_gla_gate_weights(wg2, bg):
    hk = GLA_HEADS * GLA_DK
    wg = jnp.zeros((128, 2 * hk), F32)
    for i in range(2):
        wg = wg.at[32 + i * GLA_RANK:32 + (i + 1) * GLA_RANK, i * hk:(i + 1) * hk].set(wg2[i])
    return wg.astype(BF16), bg.reshape(1, 2 * hk)


def kernel(x, c, ctx, c_ctx, ada_w, ada_b, pre_g, post_g, w_in, mla_q_norm, mla_w_uq, mla_kv_norm, mla_w_ukv, na_rpb, hy_conv_w, hy_conv_b, hy_pe_w1, hy_pe_b1, hy_pe_freq, hy_pe_w2, hy_pe_b2, hy_pe_w3, hy_skip, gla_wg2, gla_bg, gla_norm, w_branch, w_out):
    B = x.shape[0]
    depth = ada_w.shape[0]
    mod_rows = -(-(B + 1) // 8) * 8
    cc = jnp.zeros((mod_rows, D_MODEL), F32).at[:B].set(c).at[B].set(c_ctx)
    cx = ctx
    for l in range(depth):
        need_ctx = l < depth - 1
        n_tiles = N_TILES if need_ctx else N_LAT_TILES
        mod3 = _adaln(cc, ada_w[l], ada_b[l]).reshape(mod_rows, 1, 3 * D_MODEL)
        P = _inproj(x, cx, mod3, pre_g[l], _build_w_big(w_in[l]))

        mla_w = _build_mla_weights(mla_q_norm[l], mla_w_uq[l], mla_kv_norm[l], mla_w_ukv[l])
        q, k, v = _mla_prep(P, mla_w, N_TILES)
        ya = _mla_attn(q, k, v, n_tiles)

        yb = _na_attn(P, _na_bias_table(na_rpb[l]), n_tiles)

        w1p = jnp.zeros((128, HY_FFN), F32).at[:HY_EMB].set(hy_pe_w1[l])

        def hyena(L, row_blk):
            filt = _hy_filters(L, w1p, hy_pe_b1[l], hy_pe_freq[l], hy_pe_w2[l], hy_pe_b2[l], hy_pe_w3[l])
            spec, c2, s2 = _hy_spectrum(filt, L)
            return _hy_conv(P, row_blk, L, hy_conv_w[l], hy_conv_b[l], hy_skip[l], spec, c2, s2)

        yh = hyena(SEQ, 0)

        wg, bgp = _build_gla_gate_weights(gla_wg2[l], gla_bg[l])
        yg = _gla(P, wg, bgp, gla_norm[l])

        wb = w_branch[l].astype(BF16)
        wo = w_out[l].astype(BF16)
        if need_ctx:
            yhc = hyena(CTX_LEN, SEQ // CTX_LEN)
            cx = _merge(cx, P, ya, yb, yhc, yg, mod3, lambda b: B, post_g[l], wb, wo, N_LAT_TILES)
        x = _merge(x, P, ya, yb, yh, yg, mod3, lambda b: b, post_g[l], wb, wo, 0)
    return x
```

```python
import functools
import math

import numpy as np
import jax
import jax.numpy as jnp
from jax import lax
from jax.experimental import pallas as pl
from jax.experimental.pallas import tpu as pltpu

F32 = jnp.float32
BF16 = jnp.bfloat16

D_MODEL = 1024
SEQ = 2048
CTX_LEN = 256
ROWS = SEQ + CTX_LEN
GRID_W = 64
GRID_ROWS = SEQ // GRID_W
EPS = 1e-6
ROPE_BASE = 10000.0
TILE = 256
N_LAT_TILES = SEQ // TILE
N_TILES = ROWS // TILE

MLA_HEADS = 8
MLA_NOPE = 64
MLA_ROPE = 32
MLA_V = 64
MLA_Q_RANK = 256
MLA_KV_RANK = 128
MLA_SLOT = 128

NA_HEADS = 8
NA_HEAD_DIM = 64
NA_WIN_R = 8
NA_WIN_C = 16
NA_GROUP = 4
NA_KROWS = NA_GROUP + NA_WIN_R - 1
NA_NEG = -1e30

HY_WIDTH = 512
HY_EMB = 33
HY_FFN = 64
HY_TARGET = 1e-2
HY_FAST_DECAY_PCT = 0.3
HY_SLOW_DECAY_PCT = 1.5
HY_CT = 256

GLA_HEADS = 4
GLA_DK = 64
GLA_DV = 128
GLA_RANK = 16
GLA_TAU = 16.0
GLA_CHUNK = 64
GLA_NCHUNK = ROWS // GLA_CHUNK
GLA_CTX_CHUNKS = CTX_LEN // GLA_CHUNK

OFF_MERGE = 0
OFF_MLA = 4096
OFF_MLA_Z = 4608
OFF_NA_Q = 5120
OFF_NA_K = 5632
OFF_NA_V = 6144
OFF_NA_Z = 6656
OFF_HY_X1 = 7168
OFF_HY_X2 = 7680
OFF_HY_ZIN = 8192
OFF_HY_Z = 8704
OFF_GLA_QK = 9216
OFF_GLA_V = 9728
OFF_GLA_Z = 10240
P_COLS = 10752
INPROJ_TN = 512

VMEM_LIMIT = 56 * 1024 * 1024

_REF_LAYOUT = (
    ('mla_cq', 256), ('mla_ckv', 128), ('mla_kr', 32), ('mla_z', 512), ('na_qkv', 1536),
    ('na_z', 512), ('hy_proj', 1536), ('hy_z', 512), ('gla_qk', 512), ('gla_v', 512),
    ('gla_glr', 32), ('gla_z', 512), ('merge', 4096),
)


def _ref_cols(name):
    start = 0
    for n, w in _REF_LAYOUT:
        if n == name:
            return start, start + w
        start += w
    raise KeyError(name)


def _params(sem, vmem=VMEM_LIMIT):
    return pltpu.CompilerParams(dimension_semantics=sem, vmem_limit_bytes=vmem)


def _single(shape, index_map):
    return pl.BlockSpec(shape, index_map, pipeline_mode=pl.Buffered(1))


_ROPE_PERM = np.concatenate([np.arange(8, 16), np.arange(0, 8), np.arange(24, 32), np.arange(16, 24)])
_ROPE_SIGN = np.concatenate([-np.ones(8), np.ones(8), -np.ones(8), np.ones(8)]).astype(np.float32)


@functools.lru_cache(maxsize=None)
def _rope_tables():
    t = np.arange(SEQ)
    row = (t // GRID_W).astype(np.float32)
    col = (t % GRID_W).astype(np.float32)
    half = MLA_ROPE // 2
    inv = np.power(np.float32(ROPE_BASE), -np.arange(0, half, 2, dtype=np.float32) / np.float32(half))
    ar, ac = row[:, None] * inv, col[:, None] * inv
    ang = np.concatenate([ar, ar, ac, ac], axis=-1).astype(np.float32)
    cos = np.ones((ROWS, MLA_ROPE), np.float32)
    sin = np.zeros((ROWS, MLA_ROPE), np.float32)
    cos[:SEQ] = np.cos(ang)
    sin[:SEQ] = np.sin(ang)
    scale = np.float32((MLA_NOPE + MLA_ROPE) ** -0.5)
    cos_q = np.zeros((ROWS, MLA_SLOT), np.float32)
    sin_q = np.zeros((ROWS, MLA_SLOT), np.float32)
    cos_q[:, :MLA_NOPE] = scale
    cos_q[:, MLA_NOPE:MLA_NOPE + MLA_ROPE] = cos * scale
    sin_q[:, MLA_NOPE:MLA_NOPE + MLA_ROPE] = sin * scale
    cos_k = np.zeros((ROWS, MLA_SLOT), np.float32)
    sin_k = np.zeros((ROWS, MLA_SLOT), np.float32)
    cos_k[:, MLA_NOPE:MLA_NOPE + MLA_ROPE] = cos
    sin_k[:, MLA_NOPE:MLA_NOPE + MLA_ROPE] = sin
    return cos_q, sin_q, cos_k, sin_k


@functools.lru_cache(maxsize=None)
def _dft_tables(L):
    n = 2 * L
    k = np.arange(L, dtype=np.int64)
    prod = np.mod(np.outer(2 * k + 1, 2 * k + 1), 4 * n)
    ang = prod.astype(np.float64) * (math.pi / (2 * n))
    c2 = np.cos(ang).astype(np.float32)
    s2 = np.sin(ang).astype(np.float32)
    phi = (k.astype(np.float64) + 0.5) * (math.pi / n)
    cphi = np.cos(phi).astype(np.float32)[:, None]
    sphi = np.sin(phi).astype(np.float32)[:, None]
    return c2, s2, cphi, sphi


@functools.lru_cache(maxsize=None)
def _hyena_pos_features(L):
    f32 = np.float32
    t = np.arange(L, dtype=f32)
    t_norm = t / f32(max(L - 1, 1))
    bands = (HY_EMB - 1) // 2
    fr = np.linspace(1e-4, bands - 1, bands, dtype=f32)
    ang = (f32(2.0 * math.pi / L) * t[:, None] * fr[None, :]).astype(f32)
    z = np.concatenate([t_norm[:, None], np.cos(ang), -np.sin(ang)], axis=-1).astype(f32)
    zp = np.zeros((L, 128), f32)
    zp[:, :HY_EMB] = z
    max_decay = math.log(HY_TARGET) / HY_FAST_DECAY_PCT
    min_decay = math.log(HY_TARGET) / HY_SLOW_DECAY_PCT
    deltas = np.abs(np.linspace(min_decay, max_decay, HY_WIDTH, dtype=f32)).astype(f32)
    return zp, t_norm[:, None].astype(f32), deltas[None, :]


@functools.lru_cache(maxsize=None)
def _na_bias_index():
    wr = NA_WIN_R
    j = np.arange(GRID_W)
    c0 = np.clip(j - NA_WIN_C // 2, 0, GRID_W - NA_WIN_C)
    col_mask = (j[None, :] >= c0[:, None]) & (j[None, :] < c0[:, None] + NA_WIN_C)
    dcol = np.clip(j[None, :] - j[:, None], -(NA_WIN_C - 1), NA_WIN_C - 1) + NA_WIN_C - 1
    onehot = np.zeros((2 * NA_WIN_C - 1, GRID_W * GRID_W), np.float32)
    onehot[dcol.reshape(-1), np.arange(GRID_W * GRID_W)] = 1.0
    blocks = []
    for ra in (0, NA_GROUP, GRID_ROWS - NA_GROUP):
        ks = int(np.clip(ra - wr // 2, 0, GRID_ROWS - NA_KROWS))
        per_row = []
        for i in range(NA_GROUP):
            r = ra + i
            r0 = int(np.clip(r - wr // 2, 0, GRID_ROWS - wr))
            per_row.append(tuple((ks + jj - r + wr - 1) if (r0 <= ks + jj < r0 + wr) else -1
                                 for jj in range(NA_KROWS)))
        blocks.append(tuple(per_row))
    return onehot, col_mask, tuple(blocks)


def _adaln_kernel(c_ref, w_ref, b_ref, o_ref):
    c = c_ref[...]
    a = c * (1.0 / (1.0 + jnp.exp(-c)))
    o_ref[...] = jnp.dot(a, w_ref[...], preferred_element_type=F32,
                         precision=lax.Precision.HIGHEST) + b_ref[...]


def _adaln(cc, w, b):
    rows = cc.shape[0]
    tn = 768
    return pl.pallas_call(
        _adaln_kernel,
        grid=(3 * D_MODEL // tn,),
        in_specs=[pl.BlockSpec((rows, D_MODEL), lambda n: (0, 0)),
                  pl.BlockSpec((D_MODEL, tn), lambda n: (0, n)),
                  pl.BlockSpec((1, tn), lambda n: (0, n))],
        out_specs=pl.BlockSpec((rows, tn), lambda n: (0, n)),
        out_shape=jax.ShapeDtypeStruct((rows, 3 * D_MODEL), F32),
        compiler_params=_params(("arbitrary",)),
        name="adaln",
    )(cc, w, b.reshape(1, -1))


def _inproj_kernel(x_ref, cx_ref, modl_ref, modc_ref, g_ref, w_ref, o_ref, h_ref):
    @pl.when(pl.program_id(1) == 0)
    def _():
        g = g_ref[...]

        def norm_mod(xv, mod):
            ms = jnp.mean(xv * xv, axis=-1, keepdims=True)
            xn = xv * lax.rsqrt(ms + EPS) * g
            return (xn * (1.0 + mod[:, D_MODEL:2 * D_MODEL]) + mod[:, 0:D_MODEL]).astype(BF16)

        modl = modl_ref[0]
        for i in range(N_LAT_TILES):
            h_ref[i * TILE:(i + 1) * TILE, :] = norm_mod(x_ref[0, i * TILE:(i + 1) * TILE, :], modl)
        h_ref[SEQ:ROWS, :] = norm_mod(cx_ref[0], modc_ref[0])

    o_ref[0] = jnp.dot(h_ref[...], w_ref[...], preferred_element_type=F32).astype(BF16)


def _inproj(x, cx, mod3, pre_g, w_big):
    B = x.shape[0]
    return pl.pallas_call(
        _inproj_kernel,
        grid=(B, P_COLS // INPROJ_TN),
        in_specs=[pl.BlockSpec((1, SEQ, D_MODEL), lambda b, n: (b, 0, 0)),
                  pl.BlockSpec((1, CTX_LEN, D_MODEL), lambda b, n: (b, 0, 0)),
                  pl.BlockSpec((1, 1, 3 * D_MODEL), lambda b, n: (b, 0, 0)),
                  pl.BlockSpec((1, 1, 3 * D_MODEL), lambda b, n: (B, 0, 0)),
                  pl.BlockSpec((1, D_MODEL), lambda b, n: (0, 0)),
                  pl.BlockSpec((D_MODEL, INPROJ_TN), lambda b, n: (0, n))],
        out_specs=pl.BlockSpec((1, ROWS, INPROJ_TN), lambda b, n: (b, 0, n)),
        out_shape=jax.ShapeDtypeStruct((B, ROWS, P_COLS), BF16),
        scratch_shapes=[pltpu.VMEM((ROWS, D_MODEL), BF16)],
        compiler_params=_params(("arbitrary", "arbitrary")),
        name="inproj",
    )(x, cx, mod3, mod3, pre_g.reshape(1, -1), w_big)


def _mla_prep_kernel(p_ref, qn_ref, kvn_ref, wq_ref, wqx_ref, wk_ref, wv_ref, e1_ref, e2_ref,
                     cq_ref, sq_ref, ck_ref, sk_ref, q_ref, k_ref, v_ref):
    pm = p_ref[0]
    cq = pm[:, 0:MLA_Q_RANK].astype(F32)
    ckv = pm[:, MLA_Q_RANK:MLA_Q_RANK + MLA_KV_RANK].astype(F32)
    misc = pm[:, MLA_Q_RANK + MLA_KV_RANK:]
    cqn = (cq * lax.rsqrt(jnp.mean(cq * cq, axis=-1, keepdims=True) + EPS) * qn_ref[...]).astype(BF16)
    ckvn = (ckv * lax.rsqrt(jnp.mean(ckv * ckv, axis=-1, keepdims=True) + EPS) * kvn_ref[...]).astype(BF16)
    q = jnp.dot(cqn, wq_ref[...], preferred_element_type=F32)
    qx = jnp.dot(cqn, wqx_ref[...], preferred_element_type=F32)
    kn = jnp.dot(ckvn, wk_ref[...], preferred_element_type=F32)
    v = jnp.dot(ckvn, wv_ref[...], preferred_element_type=F32)
    kr = jnp.dot(misc, e1_ref[...], preferred_element_type=F32)
    krx = jnp.dot(misc, e2_ref[...], preferred_element_type=F32)
    krot = kr * ck_ref[...] + krx * sk_ref[...]
    cosq, sinq = cq_ref[...], sq_ref[...]
    for h in range(MLA_HEADS):
        sl = slice(h * MLA_SLOT, (h + 1) * MLA_SLOT)
        q_ref[0, :, sl] = (q[:, sl] * cosq + qx[:, sl] * sinq).astype(BF16)
        k_ref[0, :, sl] = (kn[:, sl] + krot).astype(BF16)
    v_ref[0] = v.astype(BF16)


def _mla_prep(P, wts, n_tiles):
    B = P.shape[0]
    width = MLA_HEADS * MLA_SLOT
    cos_q, sin_q, cos_k, sin_k = (jnp.asarray(t) for t in _rope_tables())
    const = lambda b, t: (0, 0)
    tab = pl.BlockSpec((TILE, MLA_SLOT), lambda b, t: (t, 0))
    out = pl.BlockSpec((1, TILE, width), lambda b, t: (b, t, 0))
    shp = jax.ShapeDtypeStruct((B, ROWS, width), BF16)
    return pl.pallas_call(
        _mla_prep_kernel,
        grid=(B, N_TILES),
        in_specs=[pl.BlockSpec((1, TILE, 512), lambda b, t: (b, t, OFF_MLA // 512)),
                  pl.BlockSpec((1, MLA_Q_RANK), const),
                  pl.BlockSpec((1, MLA_KV_RANK), const),
                  pl.BlockSpec((MLA_Q_RANK, width), const),
                  pl.BlockSpec((MLA_Q_RANK, width), const),
                  pl.BlockSpec((MLA_KV_RANK, width), const),
                  pl.BlockSpec((MLA_KV_RANK, width), const),
                  pl.BlockSpec((128, MLA_SLOT), const),
                  pl.BlockSpec((128, MLA_SLOT), const),
                  tab, tab, tab, tab],
        out_specs=[out, out, out],
        out_shape=[shp, shp, shp],
        compiler_params=_params(("arbitrary", "arbitrary")),
        name="mla_prep",
    )(P, wts['q_norm'], wts['kv_norm'], wts['wq'], wts['wqx'], wts['wk'], wts['wv'],
      wts['e1'], wts['e2'], cos_q, sin_q, cos_k, sin_k)


def _mla_attn_kernel(q_ref, k_ref, v_ref, o_ref):
    def attend(k0, nk):
        acc = None
        for j in range(2):
            sl = slice(j * MLA_SLOT, (j + 1) * MLA_SLOT)
            q = q_ref[0, :, sl]
            k = k_ref[0, k0:k0 + nk, sl]
            v = v_ref[0, k0:k0 + nk, sl]
            s = lax.dot_general(q, k, (((1,), (1,)), ((), ())), preferred_element_type=F32)
            m = jnp.max(s, axis=-1, keepdims=True)
            p = jnp.exp(s - m)
            l = jnp.sum(p, axis=-1, keepdims=True)
            o = jnp.dot(p.astype(BF16), v, preferred_element_type=F32) * (1.0 / l)
            acc = o if acc is None else acc + o
        o_ref[0] = acc.astype(BF16)

    is_ctx = pl.program_id(2) == N_LAT_TILES

    @pl.when(jnp.logical_not(is_ctx))
    def _():
        attend(0, ROWS)

    @pl.when(is_ctx)
    def _():
        attend(SEQ, CTX_LEN)


def _mla_attn(q, k, v, n_tiles):
    B = q.shape[0]
    pair = 2 * MLA_SLOT
    return pl.pallas_call(
        _mla_attn_kernel,
        grid=(B, MLA_HEADS // 2, n_tiles),
        in_specs=[pl.BlockSpec((1, TILE, pair), lambda b, p, t: (b, t, p)),
                  pl.BlockSpec((1, ROWS, pair), lambda b, p, t: (b, 0, p)),
                  pl.BlockSpec((1, ROWS, pair), lambda b, p, t: (b, 0, p))],
        out_specs=pl.BlockSpec((1, TILE, MLA_SLOT), lambda b, p, t: (b, t, p)),
        out_shape=jax.ShapeDtypeStruct((B, ROWS, MLA_HEADS * MLA_V), BF16),
        compiler_params=_params(("arbitrary", "arbitrary", "arbitrary")),
        name="mla_attn",
    )(q, k, v)


def _na_kernel(q_ref, k_ref, v_ref, bias_ref, o_ref):
    g = pl.program_id(1)
    lane = lax.broadcasted_iota(jnp.int32, (1, 2 * NA_HEAD_DIM), 1)
    scale = NA_HEAD_DIM ** -0.5
    nk_win = NA_KROWS * GRID_W

    def head_masks(h):
        return (lane >= NA_HEAD_DIM) if (h % 2) else (lane < NA_HEAD_DIM)

    def finish(parts, l):
        inv = 1.0 / l
        return sum(parts[1:], parts[0]) * inv

    @pl.when(g < N_LAT_TILES)
    def _():
        ks = jnp.clip(g * NA_GROUP - NA_WIN_R // 2, 0, GRID_ROWS - NA_KROWS)
        start = pl.multiple_of(ks * GRID_W, GRID_W)
        for p in range(NA_HEADS // 2):
            sl = slice(p * 128, (p + 1) * 128)
            qp = q_ref[0, :, sl] * scale
            kw = k_ref[0, pl.ds(start, nk_win), sl]
            vw = v_ref[0, pl.ds(start, nk_win), sl]
            kc = k_ref[0, SEQ:ROWS, sl]
            vc = v_ref[0, SEQ:ROWS, sl]
            acc = None
            for h in (2 * p, 2 * p + 1):
                mk = head_masks(h)
                qm = jnp.where(mk, qp, 0).astype(BF16)
                s_w = lax.dot_general(qm, kw, (((1,), (1,)), ((), ())),
                                      preferred_element_type=F32) + bias_ref[0, h]
                s_c = lax.dot_general(qm, kc, (((1,), (1,)), ((), ())), preferred_element_type=F32)
                m = jnp.maximum(jnp.max(s_w, axis=-1, keepdims=True), jnp.max(s_c, axis=-1, keepdims=True))
                p_w = jnp.exp(s_w - m)
                p_c = jnp.exp(s_c - m)
                l = jnp.sum(p_w, axis=-1, keepdims=True) + jnp.sum(p_c, axis=-1, keepdims=True)
                o = finish([jnp.dot(p_w.astype(BF16), jnp.where(mk, vw, 0).astype(BF16), preferred_element_type=F32),
                            jnp.dot(p_c.astype(BF16), jnp.where(mk, vc, 0).astype(BF16), preferred_element_type=F32)], l)
                acc = o if acc is None else acc + o
            o_ref[0, :, sl] = acc.astype(BF16)

    @pl.when(g == N_LAT_TILES)
    def _():
        for p in range(NA_HEADS // 2):
            sl = slice(p * 128, (p + 1) * 128)
            qp = q_ref[0, :, sl] * scale
            kc = k_ref[0, SEQ:ROWS, sl]
            vc = v_ref[0, SEQ:ROWS, sl]
            acc = None
            for h in (2 * p, 2 * p + 1):
                mk = head_masks(h)
                qm = jnp.where(mk, qp, 0).astype(BF16)
                s_c = lax.dot_general(qm, kc, (((1,), (1,)), ((), ())), preferred_element_type=F32)
                m = jnp.max(s_c, axis=-1, keepdims=True)
                p_c = jnp.exp(s_c - m)
                l = jnp.sum(p_c, axis=-1, keepdims=True)
                o = finish([jnp.dot(p_c.astype(BF16), jnp.where(mk, vc, 0).astype(BF16), preferred_element_type=F32)], l)
                acc = o if acc is None else acc + o
            o_ref[0, :, sl] = acc.astype(BF16)


def _na_attn(P, bias_tab, n_tiles):
    B = P.shape[0]
    width = NA_HEADS * NA_HEAD_DIM
    nq, nk = NA_GROUP * GRID_W, NA_KROWS * GRID_W

    def bias_map(b, g):
        return (jnp.where(g == 0, 0, jnp.where(g >= N_LAT_TILES - 1, 2, 1)), 0, 0, 0)

    return pl.pallas_call(
        _na_kernel,
        grid=(B, n_tiles),
        in_specs=[pl.BlockSpec((1, TILE, width), lambda b, g: (b, g, OFF_NA_Q // width)),
                  pl.BlockSpec((1, ROWS, width), lambda b, g: (b, 0, OFF_NA_K // width)),
                  pl.BlockSpec((1, ROWS, width), lambda b, g: (b, 0, OFF_NA_V // width)),
                  pl.BlockSpec((1, NA_HEADS, nq, nk), bias_map)],
        out_specs=pl.BlockSpec((1, TILE, width), lambda b, g: (b, g, 0)),
        out_shape=jax.ShapeDtypeStruct((B, ROWS, width), BF16),
        compiler_params=_params(("arbitrary", "arbitrary")),
        name="na_attn",
    )(P, P, P, bias_tab)


def _na_bias_table(rpb):
    onehot, col_mask, drow_blocks = _na_bias_index()
    rc = jnp.dot(rpb.astype(F32).reshape(-1, 2 * NA_WIN_C - 1), jnp.asarray(onehot),
                 precision=lax.Precision.HIGHEST)
    rc = rc.reshape(NA_HEADS, 2 * NA_WIN_R - 1, GRID_W, GRID_W)
    rc = jnp.where(col_mask[None, None], rc, NA_NEG)
    neg = jnp.full((NA_HEADS, GRID_W, GRID_W), NA_NEG, F32)
    pats = []
    for p in range(3):
        rows = []
        for i in range(NA_GROUP):
            rows.append(jnp.concatenate(
                [neg if d < 0 else rc[:, d] for d in drow_blocks[p][i]], axis=-1))
        pats.append(jnp.concatenate(rows, axis=1))
    return jnp.stack(pats, axis=0)


def _hy_filter_kernel(z_ref, tn_ref, dl_ref, w1_ref, b1_ref, fr_ref, w2_ref, b2_ref, w3_ref, o_ref):
    hp = lax.Precision.HIGHEST
    fr = fr_ref[...]
    a = jnp.sin(fr[0:1, :] * (jnp.dot(z_ref[...], w1_ref[...], preferred_element_type=F32, precision=hp)
                              + b1_ref[...]))
    a = jnp.sin(fr[1:2, :] * (jnp.dot(a, w2_ref[...], preferred_element_type=F32, precision=hp)
                              + b2_ref[...]))
    filt = jnp.dot(a, w3_ref[...], preferred_element_type=F32, precision=hp)
    window = jnp.exp(-tn_ref[...] * dl_ref[...])
    f0 = filt[:, 0:HY_CT] * window
    f1 = filt[:, HY_CT:] * window
    den = (jnp.sum(jnp.abs(f0), axis=0, keepdims=True) + jnp.sum(jnp.abs(f1), axis=0, keepdims=True)) + EPS
    o_ref[:, 0:HY_CT] = f0 / den
    o_ref[:, HY_CT:] = f1 / den


def _hy_filters(L, w1p, b1, freq, w2, b2, w3):
    zp, t_norm, deltas = (jnp.asarray(t) for t in _hyena_pos_features(L))
    n_ct = HY_WIDTH // HY_CT
    w3t = w3.reshape(HY_FFN, 2, 2, n_ct, HY_CT).transpose(0, 1, 3, 2, 4).reshape(HY_FFN, 4 * HY_WIDTH)
    const = lambda n: (0, 0)
    return pl.pallas_call(
        _hy_filter_kernel,
        grid=(2 * n_ct,),
        in_specs=[pl.BlockSpec((L, 128), const), pl.BlockSpec((L, 1), const),
                  pl.BlockSpec((1, HY_CT), lambda n: (0, n % n_ct)), pl.BlockSpec((128, HY_FFN), const),
                  pl.BlockSpec((1, HY_FFN), const), pl.BlockSpec((2, HY_FFN), const),
                  pl.BlockSpec((HY_FFN, HY_FFN), const), pl.BlockSpec((1, HY_FFN), const),
                  pl.BlockSpec((HY_FFN, 2 * HY_CT), lambda n: (0, n))],
        out_specs=pl.BlockSpec((L, 2 * HY_CT), lambda n: (0, n)),
        out_shape=jax.ShapeDtypeStruct((L, 4 * HY_WIDTH), F32),
        compiler_params=_params(("arbitrary",)),
        name="hyena_filters",
    )(zp, t_norm, deltas, w1p, b1.reshape(1, -1), freq, w2, b2.reshape(1, -1), w3t)


def _hy_spectrum_kernel(f_ref, c2_ref, s2_ref, cp_ref, sp_ref, o_ref, *, L):
    half = f_ref.shape[1] // 2
    row = lax.broadcasted_iota(jnp.int32, (L, half), 0)
    ff = f_ref[:, 0:half]
    fb = jnp.where(row == 0, 0.0, f_ref[:, half:])

    def split(x):
        hi = x.astype(BF16)
        return hi, (x - hi.astype(F32)).astype(BF16)

    def xform(m_ref, parts):
        return (jnp.dot(m_ref[...], parts[0], preferred_element_type=F32)
                + jnp.dot(m_ref[...], parts[1], preferred_element_type=F32))

    pf, pb = split(ff), split(fb)
    a_f, b_f = xform(c2_ref, pf), xform(s2_ref, pf)
    a_b, b_b = xform(c2_ref, pb), xform(s2_ref, pb)
    cp, sp = cp_ref[...], sp_ref[...]
    scale = 1.0 / L
    o_ref[0] = (cp * (a_f + a_b) + sp * (b_f + b_b)) * scale
    o_ref[1] = (sp * (a_f - a_b) - cp * (b_f - b_b)) * scale


def _hy_spectrum(filt, L):
    c2, s2, cphi, sphi = _dft_tables(L)
    c2, s2 = jnp.asarray(c2, BF16), jnp.asarray(s2, BF16)
    n_ct = HY_WIDTH // HY_CT
    ft = min(L, 512)
    out = pl.pallas_call(
        functools.partial(_hy_spectrum_kernel, L=L),
        grid=(2 * n_ct, L // ft),
        in_specs=[pl.BlockSpec((L, 2 * HY_CT), lambda n, f: (0, n)),
                  pl.BlockSpec((ft, L), lambda n, f: (f, 0)),
                  pl.BlockSpec((ft, L), lambda n, f: (f, 0)),
                  pl.BlockSpec((ft, 1), lambda n, f: (f, 0)),
                  pl.BlockSpec((ft, 1), lambda n, f: (f, 0))],
        out_specs=pl.BlockSpec((2, ft, HY_CT), lambda n, f: (0, f, n)),
        out_shape=jax.ShapeDtypeStruct((2, L, 2 * n_ct * HY_CT), F32),
        compiler_params=_params(("arbitrary", "arbitrary")),
        name="hyena_spectrum",
    )(filt, c2, s2, jnp.asarray(cphi), jnp.asarray(sphi))
    return out, c2, s2


def _hy_conv_kernel(x1_ref, x2_ref, z_ref, w_ref, b_ref, skip_ref, h_ref, c2_ref, s2_ref, o_ref,
                    yre_ref, yim_ref, *, L):
    row = lax.broadcasted_iota(jnp.int32, (L, HY_CT), 0)

    def short_conv(p_ref, i):
        p = p_ref[0].astype(F32)
        w = w_ref[i]
        prev = jnp.where(row == 0, 0.0, pltpu.roll(p, 1, 0))
        nxt = jnp.where(row == L - 1, 0.0, pltpu.roll(p, L - 1, 0))
        return prev * w[0:1, :] + p * w[1:2, :] + nxt * w[2:3, :] + b_ref[i]

    ft = min(L, 512)
    z = short_conv(z_ref, 2)
    for n, gate_ref in enumerate((x1_ref, x2_ref)):
        zb = z.astype(BF16)
        for f in range(L // ft):
            fs = slice(f * ft, (f + 1) * ft)
            ur = jnp.dot(c2_ref[fs, :], zb, preferred_element_type=F32)
            us = jnp.dot(s2_ref[fs, :], zb, preferred_element_type=F32)
            hre, him = h_ref[0, n, fs, :], h_ref[1, n, fs, :]
            yre_ref[fs, :] = (ur * hre + us * him).astype(BF16)
            yim_ref[fs, :] = (ur * him - us * hre).astype(BF16)
        y = (jnp.dot(c2_ref[...], yre_ref[...], preferred_element_type=F32)
             - jnp.dot(s2_ref[...], yim_ref[...], preferred_element_type=F32))
        z = short_conv(gate_ref, n) * (y + z * skip_ref[n])
    o_ref[0] = z.astype(BF16)


def _hy_conv(P, row_blk, L, conv_w, conv_b, skip, spec, c2, s2):
    B = P.shape[0]
    n_ct = HY_WIDTH // HY_CT
    w4 = conv_w.reshape(3, 3, n_ct, HY_CT).transpose(2, 1, 0, 3)
    b4 = conv_b.reshape(3, n_ct, 1, HY_CT).transpose(1, 0, 2, 3)
    sk4 = skip.reshape(2, n_ct, 1, HY_CT).transpose(1, 0, 2, 3)
    h5 = spec.reshape(2, L, 2, n_ct, HY_CT).transpose(3, 0, 2, 1, 4)

    def pcol(off):
        return pl.BlockSpec((1, L, HY_CT), lambda c, b: (b, row_blk, off // HY_CT + c))

    const = lambda c, b: (0, 0)
    return pl.pallas_call(
        functools.partial(_hy_conv_kernel, L=L),
        grid=(n_ct, B),
        in_specs=[pcol(OFF_HY_X1), pcol(OFF_HY_X2), pcol(OFF_HY_ZIN),
                  pl.BlockSpec((None, 3, 3, HY_CT), lambda c, b: (c, 0, 0, 0)),
                  pl.BlockSpec((None, 3, 1, HY_CT), lambda c, b: (c, 0, 0, 0)),
                  pl.BlockSpec((None, 2, 1, HY_CT), lambda c, b: (c, 0, 0, 0)),
                  pl.BlockSpec((None, 2, 2, L, HY_CT), lambda c, b: (c, 0, 0, 0, 0),
                               pipeline_mode=pl.Buffered(1)),
                  _single((L, L), const), _single((L, L), const)],
        out_specs=pl.BlockSpec((1, L, HY_CT), lambda c, b: (b, 0, c)),
        out_shape=jax.ShapeDtypeStruct((B, L, HY_WIDTH), BF16),
        scratch_shapes=[pltpu.VMEM((L, HY_CT), BF16), pltpu.VMEM((L, HY_CT), BF16)],
        compiler_params=_params(("arbitrary", "arbitrary")),
        name="hyena_conv",
    )(P, P, P, w4, b4, sk4, h5, c2, s2)


def _gla_kernel(qk_ref, v_ref, misc_ref, wg_ref, bg_ref, ng_ref, o_ref, g_scr, o_scr, s_scr):
    C = GLA_CHUNK
    HK = GLA_HEADS * GLA_DK
    logit = jnp.dot(misc_ref[0], wg_ref[...], preferred_element_type=F32) + bg_ref[...]
    g_scr[...] = (jnp.minimum(logit, 0.0) - jnp.log(1.0 + jnp.exp(-jnp.abs(logit)))) * (1.0 / GLA_TAU)
    s_scr[...] = jnp.zeros_like(s_scr)

    ri = lax.broadcasted_iota(jnp.int32, (C, C), 0)
    ci = lax.broadcasted_iota(jnp.int32, (C, C), 1)
    lane = lax.broadcasted_iota(jnp.int32, (1, 2 * GLA_DK), 1)
    lower = ri >= ci
    tri = (lower.astype(F32), jnp.logical_not(ri > ci).astype(F32))
    keep = (lower, ri <= ci)
    last = (C - 1, 0)
    mid = (C // 2 - 1, C // 2)

    def one_chunk(d, c):
        r0 = pl.multiple_of(c * C, C)
        rows = pl.ds(r0, C)
        g = g_scr[rows, d * HK:(d + 1) * HK]
        bc = jnp.dot(tri[d], g, preferred_element_type=F32, precision=lax.Precision.HIGHEST)
        bl = bc[last[d]:last[d] + 1, :]
        bm = bc[mid[d]:mid[d] + 1, :]
        q = qk_ref[0, rows, 0:HK].astype(F32) * (GLA_DK ** -0.5)
        k = qk_ref[0, rows, HK:2 * HK].astype(F32)
        qe = (q * jnp.exp(bc - bm)).astype(BF16)
        ke = (k * jnp.exp(bm - bc)).astype(BF16)
        qi = (q * jnp.exp(bc)).astype(BF16)
        ko = (k * jnp.exp(bl - bc)).astype(BF16)
        dec = jnp.exp(bl)
        for h in range(GLA_HEADS):
            p = h // 2
            sl = slice(p * 128, (p + 1) * 128)
            mk = (lane >= GLA_DK) if (h % 2) else (lane < GLA_DK)
            v_h = v_ref[0, rows, h * GLA_DV:(h + 1) * GLA_DV]
            qe_h = jnp.where(mk, qe[:, sl], 0).astype(BF16)
            a = lax.dot_general(qe_h, ke[:, sl], (((1,), (1,)), ((), ())), preferred_element_type=F32)
            a = jnp.where(keep[d], a, 0.0).astype(BF16)
            st = s_scr[d, h]
            qi_h = jnp.where(mk, qi[:, sl], 0).astype(BF16)
            o = (jnp.dot(a, v_h, preferred_element_type=F32)
                 + lax.dot_general(qi_h, st.astype(BF16), (((1,), (1,)), ((), ())),
                                   preferred_element_type=F32))
            ko_h = jnp.where(mk, ko[:, sl], 0).astype(BF16)
            kv_t = lax.dot_general(v_h, ko_h, (((0,), (0,)), ((), ())), preferred_element_type=F32)
            s_scr[d, h] = st * dec[:, sl] + kv_t
            o_scr[d, rows, h * GLA_DV:(h + 1) * GLA_DV] = o

    def step(i, carry):
        cf = jnp.where(i < GLA_CTX_CHUNKS, GLA_NCHUNK - GLA_CTX_CHUNKS + i, i - GLA_CTX_CHUNKS)
        cb = GLA_NCHUNK - 1 - i
        one_chunk(0, cf)
        one_chunk(1, cb)
        return carry

    lax.fori_loop(0, GLA_NCHUNK, step, 0)

    ng = ng_ref[...]
    for t in range(N_TILES):
        rs = slice(t * TILE, (t + 1) * TILE)
        for h in range(GLA_HEADS):
            cs = slice(h * GLA_DV, (h + 1) * GLA_DV)
            o = o_scr[0, rs, cs] + o_scr[1, rs, cs]
            o = o * lax.rsqrt(jnp.mean(o * o, axis=-1, keepdims=True) + EPS) * ng
            o_ref[0, rs, cs] = o.astype(BF16)


def _gla(P, wg, bg, norm_g):
    B = P.shape[0]
    width = GLA_HEADS * GLA_DV
    const = lambda b: (0, 0)
    return pl.pallas_call(
        _gla_kernel,
        grid=(B,),
        in_specs=[pl.BlockSpec((1, ROWS, 512), lambda b: (b, 0, OFF_GLA_QK // 512)),
                  pl.BlockSpec((1, ROWS, width), lambda b: (b, 0, OFF_GLA_V // width)),
                  pl.BlockSpec((1, ROWS, 128), lambda b: (b, 0, (OFF_MLA + 384) // 128)),
                  pl.BlockSpec((128, 2 * GLA_HEADS * GLA_DK), const),
                  pl.BlockSpec((1, 2 * GLA_HEADS * GLA_DK), const),
                  pl.BlockSpec((1, GLA_DV), const)],
        out_specs=pl.BlockSpec((1, ROWS, width), lambda b: (b, 0, 0)),
        out_shape=jax.ShapeDtypeStruct((B, ROWS, width), BF16),
        scratch_shapes=[pltpu.VMEM((ROWS, 2 * GLA_HEADS * GLA_DK), F32),
                        pltpu.VMEM((2, ROWS, width), F32),
                        pltpu.VMEM((2, GLA_HEADS, GLA_DV, 2 * GLA_DK), F32)],
        compiler_params=_params(("arbitrary",)),
        name="gla",
    )(P, P, P, wg, bg, norm_g.reshape(1, -1))


def _merge_kernel(ya_ref, yb_ref, yh_ref, yg_ref, za_ref, zb_ref, zh_ref, zg_ref, gt_ref,
                  x_ref, mod_ref, pg_ref, wb_ref, wo_ref, o_ref):
    merged = None
    for i, (y_ref, z_ref) in enumerate(((ya_ref, za_ref), (yb_ref, zb_ref), (yh_ref, zh_ref), (yg_ref, zg_ref))):
        z = z_ref[0].astype(F32)
        t = (y_ref[0].astype(F32) * (z * (1.0 / (1.0 + jnp.exp(-z))))).astype(BF16)
        br = jnp.dot(t, wb_ref[i], preferred_element_type=F32)
        gl = gt_ref[0, :, i * D_MODEL:(i + 1) * D_MODEL].astype(F32)
        term = br * (1.0 / (1.0 + jnp.exp(-gl)))
        merged = term if merged is None else merged + term
    out = jnp.dot(merged.astype(BF16), wo_ref[...], preferred_element_type=F32)
    xn = out * lax.rsqrt(jnp.mean(out * out, axis=-1, keepdims=True) + EPS) * pg_ref[...]
    gate = mod_ref[0][:, 2 * D_MODEL:3 * D_MODEL]
    o_ref[0] = x_ref[0] + gate * xn


def _merge(x, P, ya, yb, yh, yg, mod3, mod_row, post_g, wb, wo, row_off):
    B, R, _ = x.shape
    n_t = R // TILE
    const2 = lambda b, t: (0, 0)

    def uni(width, off):
        return pl.BlockSpec((1, TILE, width), lambda b, t: (b, t + row_off, off // width))

    return pl.pallas_call(
        _merge_kernel,
        grid=(B, n_t),
        in_specs=[uni(512, 0), uni(512, 0),
                  pl.BlockSpec((1, TILE, 512), lambda b, t: (b, t, 0)),
                  uni(512, 0),
                  uni(512, OFF_MLA_Z), uni(512, OFF_NA_Z), uni(512, OFF_HY_Z), uni(512, OFF_GLA_Z),
                  uni(4 * D_MODEL, OFF_MERGE),
                  pl.BlockSpec((1, TILE, D_MODEL), lambda b, t: (b, t, 0)),
                  pl.BlockSpec((1, 1, 3 * D_MODEL), lambda b, t: (mod_row(b), 0, 0)),
                  pl.BlockSpec((1, D_MODEL), const2),
                  pl.BlockSpec((4, 512, D_MODEL), lambda b, t: (0, 0, 0)),
                  pl.BlockSpec((D_MODEL, D_MODEL), const2)],
        out_specs=pl.BlockSpec((1, TILE, D_MODEL), lambda b, t: (b, t, 0)),
        out_shape=jax.ShapeDtypeStruct((B, R, D_MODEL), F32),
        compiler_params=_params(("arbitrary", "arbitrary")),
        name="merge",
    )(ya, yb, yh, yg, P, P, P, P, P, x, mod3, post_g.reshape(1, -1), wb, wo)


def _build_w_big(w):
    def cols(name):
        a, b = _ref_cols(name)
        return w[:, a:b]

    kr = cols('mla_kr')
    krx = kr[:, _ROPE_PERM] * _ROPE_SIGN
    pad32 = jnp.zeros((D_MODEL, 32), w.dtype)
    parts = [cols('merge'), cols('mla_cq'), cols('mla_ckv'), kr, cols('gla_glr'), krx, pad32,
             cols('mla_z'), cols('na_qkv'), cols('na_z'), cols('hy_proj'), cols('hy_z'),
             cols('gla_qk'), cols('gla_v'), cols('gla_z')]
    big = jnp.concatenate(parts, axis=1).astype(BF16)
    assert big.shape == (D_MODEL, P_COLS)
    return big


def _build_mla_weights(q_norm, w_uq, kv_norm, w_ukv):
    dq = MLA_NOPE + MLA_ROPE
    wq3 = w_uq.reshape(MLA_Q_RANK, MLA_HEADS, dq)
    rot = wq3[:, :, MLA_NOPE:]
    rotx = rot[:, :, _ROPE_PERM] * _ROPE_SIGN
    zq = jnp.zeros((MLA_Q_RANK, MLA_HEADS, MLA_SLOT - dq), F32)
    wq = jnp.concatenate([wq3, zq], axis=-1).reshape(MLA_Q_RANK, MLA_HEADS * MLA_SLOT)
    wqx = jnp.concatenate([jnp.zeros((MLA_Q_RANK, MLA_HEADS, MLA_NOPE), F32), rotx, zq],
                          axis=-1).reshape(MLA_Q_RANK, MLA_HEADS * MLA_SLOT)
    wkv3 = w_ukv.reshape(MLA_KV_RANK, MLA_HEADS, MLA_NOPE + MLA_V)
    zk = jnp.zeros((MLA_KV_RANK, MLA_HEADS, MLA_SLOT - MLA_NOPE), F32)
    wk = jnp.concatenate([wkv3[:, :, :MLA_NOPE], zk], axis=-1).reshape(MLA_KV_RANK, MLA_HEADS * MLA_SLOT)
    wv3 = wkv3[:, :, MLA_NOPE:]
    zv = jnp.zeros_like(wv3)
    even = jnp.concatenate([wv3, zv], axis=-1)
    odd = jnp.concatenate([zv, wv3], axis=-1)
    sel = (np.arange(MLA_HEADS) % 2 == 1)[None, :, None]
    wv = jnp.where(sel, odd, even).reshape(MLA_KV_RANK, MLA_HEADS * MLA_SLOT)
    e1 = np.zeros((128, MLA_SLOT), np.float32)
    e2 = np.zeros((128, MLA_SLOT), np.float32)
    for i in range(MLA_ROPE):
        e1[i, MLA_NOPE + i] = 1.0
        e2[64 + i, MLA_NOPE + i] = 1.0
    return dict(q_norm=q_norm.reshape(1, -1), kv_norm=kv_norm.reshape(1, -1),
                wq=wq.astype(BF16), wqx=wqx.astype(BF16), wk=wk.astype(BF16), wv=wv.astype(BF16),
                e1=jnp.asarray(e1, BF16), e2=jnp.asarray(e2, BF16))


def _build_gla_gate_weights(wg2, bg):
    hk = GLA_HEADS * GLA_DK
    wg = jnp.zeros((128, 2 * hk), F32)
    for i in range(2):
        wg = wg.at[32 + i * GLA_RANK:32 + (i + 1) * GLA_RANK, i * hk:(i + 1) * hk].set(wg2[i])
    return wg.astype(BF16), bg.reshape(1, 2 * hk)


def kernel(x, c, ctx, c_ctx, ada_w, ada_b, pre_g, post_g, w_in, mla_q_norm, mla_w_uq, mla_kv_norm, mla_w_ukv, na_rpb, hy_conv_w, hy_conv_b, hy_pe_w1, hy_pe_b1, hy_pe_freq, hy_pe_w2, hy_pe_b2, hy_pe_w3, hy_skip, gla_wg2, gla_bg, gla_norm, w_branch, w_out):
    B = x.shape[0]
    depth = ada_w.shape[0]
    mod_rows = -(-(B + 1) // 8) * 8
    cc = jnp.zeros((mod_rows, D_MODEL), F32).at[:B].set(c).at[B].set(c_ctx)
    cx = ctx
    for l in range(depth):
        need_ctx = l < depth - 1
        n_tiles = N_TILES if need_ctx else N_LAT_TILES
        mod3 = _adaln(cc, ada_w[l], ada_b[l]).reshape(mod_rows, 1, 3 * D_MODEL)
        P = _inproj(x, cx, mod3, pre_g[l], _build_w_big(w_in[l]))

        mla_w = _build_mla_weights(mla_q_norm[l], mla_w_uq[l], mla_kv_norm[l], mla_w_ukv[l])
        q, k, v = _mla_prep(P, mla_w, N_TILES)
        ya = _mla_attn(q, k, v, n_tiles)

        yb = _na_attn(P, _na_bias_table(na_rpb[l]), n_tiles)

        w1p = jnp.zeros((128, HY_FFN), F32).at[:HY_EMB].set(hy_pe_w1[l])

        def hyena(L, row_blk):
            filt = _hy_filters(L, w1p, hy_pe_b1[l], hy_pe_freq[l], hy_pe_w2[l], hy_pe_b2[l], hy_pe_w3[l])
            spec, c2, s2 = _hy_spectrum(filt, L)
            return _hy_conv(P, row_blk, L, hy_conv_w[l], hy_conv_b[l], hy_skip[l], spec, c2, s2)

        yh = hyena(SEQ, 0)

        wg, bgp = _build_gla_gate_weights(gla_wg2[l], gla_bg[l])
        yg = _gla(P, wg, bgp, gla_norm[l])

        wb = w_branch[l].astype(BF16)
        wo = w_out[l].astype(BF16)
        if need_ctx:
            yhc = hyena(CTX_LEN, SEQ // CTX_LEN)
            cx = _merge(cx, P, ya, yb, yhc, yg, mod3, lambda b: B, post_g[l], wb, wo, N_LAT_TILES)
        x = _merge(x, P, ya, yb, yh, yg, mod3, lambda b: b, post_g[l], wb, wo, 0)
    return x
```

```python
import functools
import math

import numpy as np
import jax
import jax.numpy as jnp
from jax import lax
from jax.experimental import pallas as pl
from jax.experimental.pallas import tpu as pltpu

F32 = jnp.float32
BF16 = jnp.bfloat16

D_MODEL = 1024
SEQ = 2048
CTX_LEN = 256
ROWS = SEQ + CTX_LEN
GRID_W = 64
GRID_ROWS = SEQ // GRID_W
EPS = 1e-6
ROPE_BASE = 10000.0
TILE = 256
N_LAT_TILES = SEQ // TILE
N_TILES = ROWS // TILE

MLA_HEADS = 8
MLA_NOPE = 64
MLA_ROPE = 32
MLA_V = 64
MLA_Q_RANK = 256
MLA_KV_RANK = 128
MLA_SLOT = 128
MLA_HEADS_PER_STEP = 4

NA_HEADS = 8
NA_HEAD_DIM = 64
NA_WIN_R = 8
NA_WIN_C = 16
NA_GROUP = 4
NA_KROWS = NA_GROUP + NA_WIN_R - 1
NA_NEG = -1e30

HY_WIDTH = 512
HY_EMB = 33
HY_FFN = 64
HY_TARGET = 1e-2
HY_FAST_DECAY_PCT = 0.3
HY_SLOW_DECAY_PCT = 1.5
HY_CT = 256

GLA_HEADS = 4
GLA_DK = 64
GLA_DV = 128
GLA_RANK = 16
GLA_TAU = 16.0
GLA_CHUNK = 64
GLA_NCHUNK = ROWS // GLA_CHUNK
GLA_CTX_CHUNKS = CTX_LEN // GLA_CHUNK

OFF_MERGE = 0
OFF_MLA = 4096
OFF_MLA_Z = 4608
OFF_NA_Q = 5120
OFF_NA_K = 5632
OFF_NA_V = 6144
OFF_NA_Z = 6656
OFF_HY_X1 = 7168
OFF_HY_X2 = 7680
OFF_HY_ZIN = 8192
OFF_HY_Z = 8704
OFF_GLA_QK = 9216
OFF_GLA_V = 9728
OFF_GLA_Z = 10240
P_COLS = 10752
INPROJ_TN = 512

VMEM_LIMIT = 56 * 1024 * 1024

_REF_LAYOUT = (
    ('mla_cq', 256), ('mla_ckv', 128), ('mla_kr', 32), ('mla_z', 512), ('na_qkv', 1536),
    ('na_z', 512), ('hy_proj', 1536), ('hy_z', 512), ('gla_qk', 512), ('gla_v', 512),
    ('gla_glr', 32), ('gla_z', 512), ('merge', 4096),
)


def _ref_cols(name):
    start = 0
    for n, w in _REF_LAYOUT:
        if n == name:
            return start, start + w
        start += w
    raise KeyError(name)


def _params(sem, vmem=VMEM_LIMIT):
    return pltpu.CompilerParams(dimension_semantics=sem, vmem_limit_bytes=vmem)


def _single(shape, index_map):
    return pl.BlockSpec(shape, index_map, pipeline_mode=pl.Buffered(1))


_ROPE_PERM = np.concatenate([np.arange(8, 16), np.arange(0, 8), np.arange(24, 32), np.arange(16, 24)])
_ROPE_SIGN = np.concatenate([-np.ones(8), np.ones(8), -np.ones(8), np.ones(8)]).astype(np.float32)


@functools.lru_cache(maxsize=None)
def _rope_tables():
    t = np.arange(SEQ)
    row = (t // GRID_W).astype(np.float32)
    col = (t % GRID_W).astype(np.float32)
    half = MLA_ROPE // 2
    inv = np.power(np.float32(ROPE_BASE), -np.arange(0, half, 2, dtype=np.float32) / np.float32(half))
    ar, ac = row[:, None] * inv, col[:, None] * inv
    ang = np.concatenate([ar, ar, ac, ac], axis=-1).astype(np.float32)
    cos = np.ones((ROWS, MLA_ROPE), np.float32)
    sin = np.zeros((ROWS, MLA_ROPE), np.float32)
    cos[:SEQ] = np.cos(ang)
    sin[:SEQ] = np.sin(ang)
    scale = np.float32((MLA_NOPE + MLA_ROPE) ** -0.5 * math.log2(math.e))
    cos_q = np.zeros((ROWS, MLA_SLOT), np.float32)
    sin_q = np.zeros((ROWS, MLA_SLOT), np.float32)
    cos_q[:, :MLA_NOPE] = scale
    cos_q[:, MLA_NOPE:MLA_NOPE + MLA_ROPE] = cos * scale
    sin_q[:, MLA_NOPE:MLA_NOPE + MLA_ROPE] = sin * scale
    cos_k = np.zeros((ROWS, MLA_SLOT), np.float32)
    sin_k = np.zeros((ROWS, MLA_SLOT), np.float32)
    cos_k[:, MLA_NOPE:MLA_NOPE + MLA_ROPE] = cos
    sin_k[:, MLA_NOPE:MLA_NOPE + MLA_ROPE] = sin
    return cos_q, sin_q, cos_k, sin_k


@functools.lru_cache(maxsize=None)
def _dft_tables(L):
    n = 2 * L
    k = np.arange(L, dtype=np.int64)
    prod = np.mod(np.outer(2 * k + 1, 2 * k + 1), 4 * n)
    ang = prod.astype(np.float64) * (math.pi / (2 * n))
    c2 = np.cos(ang).astype(np.float32)
    s2 = np.sin(ang).astype(np.float32)
    phi = (k.astype(np.float64) + 0.5) * (math.pi / n)
    cphi = np.cos(phi).astype(np.float32)[:, None]
    sphi = np.sin(phi).astype(np.float32)[:, None]
    return c2, s2, cphi, sphi


@functools.lru_cache(maxsize=None)
def _hyena_pos_features(L):
    f32 = np.float32
    t = np.arange(L, dtype=f32)
    t_norm = t / f32(max(L - 1, 1))
    bands = (HY_EMB - 1) // 2
    fr = np.linspace(1e-4, bands - 1, bands, dtype=f32)
    ang = (f32(2.0 * math.pi / L) * t[:, None] * fr[None, :]).astype(f32)
    z = np.concatenate([t_norm[:, None], np.cos(ang), -np.sin(ang)], axis=-1).astype(f32)
    zp = np.zeros((L, 128), f32)
    zp[:, :HY_EMB] = z
    max_decay = math.log(HY_TARGET) / HY_FAST_DECAY_PCT
    min_decay = math.log(HY_TARGET) / HY_SLOW_DECAY_PCT
    deltas = np.abs(np.linspace(min_decay, max_decay, HY_WIDTH, dtype=f32)).astype(f32)
    return zp, t_norm[:, None].astype(f32), deltas[None, :]


@functools.lru_cache(maxsize=None)
def _na_bias_index():
    wr = NA_WIN_R
    j = np.arange(GRID_W)
    c0 = np.clip(j - NA_WIN_C // 2, 0, GRID_W - NA_WIN_C)
    col_mask = (j[None, :] >= c0[:, None]) & (j[None, :] < c0[:, None] + NA_WIN_C)
    dcol = np.clip(j[None, :] - j[:, None], -(NA_WIN_C - 1), NA_WIN_C - 1) + NA_WIN_C - 1
    onehot = np.zeros((2 * NA_WIN_C - 1, GRID_W * GRID_W), np.float32)
    onehot[dcol.reshape(-1), np.arange(GRID_W * GRID_W)] = 1.0
    blocks = []
    for ra in (0, NA_GROUP, GRID_ROWS - NA_GROUP):
        ks = int(np.clip(ra - wr // 2, 0, GRID_ROWS - NA_KROWS))
        per_row = []
        for i in range(NA_GROUP):
            r = ra + i
            r0 = int(np.clip(r - wr // 2, 0, GRID_ROWS - wr))
            per_row.append(tuple((ks + jj - r + wr - 1) if (r0 <= ks + jj < r0 + wr) else -1
                                 for jj in range(NA_KROWS)))
        blocks.append(tuple(per_row))
    return onehot, col_mask, tuple(blocks)


def _adaln_kernel(c_ref, w_ref, b_ref, o_ref):
    c = c_ref[...]
    a = c * (1.0 / (1.0 + jnp.exp(-c)))
    o_ref[...] = jnp.dot(a, w_ref[...], preferred_element_type=F32,
                         precision=lax.Precision.HIGHEST) + b_ref[...]


def _adaln(cc, w, b):
    rows = cc.shape[0]
    tn = 768
    return pl.pallas_call(
        _adaln_kernel,
        grid=(3 * D_MODEL // tn,),
        in_specs=[pl.BlockSpec((rows, D_MODEL), lambda n: (0, 0)),
                  pl.BlockSpec((D_MODEL, tn), lambda n: (0, n)),
                  pl.BlockSpec((1, tn), lambda n: (0, n))],
        out_specs=pl.BlockSpec((rows, tn), lambda n: (0, n)),
        out_shape=jax.ShapeDtypeStruct((rows, 3 * D_MODEL), F32),
        compiler_params=_params(("arbitrary",)),
        name="adaln",
    )(cc, w, b.reshape(1, -1))


def _inproj_kernel(x_ref, cx_ref, modl_ref, modc_ref, g_ref, w_ref, o_ref, h_ref):
    @pl.when(pl.program_id(1) == 0)
    def _():
        g = g_ref[...]

        def norm_mod(xv, mod):
            ms = jnp.mean(xv * xv, axis=-1, keepdims=True)
            xn = xv * lax.rsqrt(ms + EPS) * g
            return (xn * (1.0 + mod[:, D_MODEL:2 * D_MODEL]) + mod[:, 0:D_MODEL]).astype(BF16)

        modl = modl_ref[0]
        for i in range(N_LAT_TILES):
            h_ref[i * TILE:(i + 1) * TILE, :] = norm_mod(x_ref[0, i * TILE:(i + 1) * TILE, :], modl)
        h_ref[SEQ:ROWS, :] = norm_mod(cx_ref[0], modc_ref[0])

    o_ref[0] = jnp.dot(h_ref[...], w_ref[...], preferred_element_type=F32).astype(BF16)


def _inproj(x, cx, mod3, pre_g, w_big):
    B = x.shape[0]
    return pl.pallas_call(
        _inproj_kernel,
        grid=(B, P_COLS // INPROJ_TN),
        in_specs=[pl.BlockSpec((1, SEQ, D_MODEL), lambda b, n: (b, 0, 0)),
                  pl.BlockSpec((1, CTX_LEN, D_MODEL), lambda b, n: (b, 0, 0)),
                  pl.BlockSpec((1, 1, 3 * D_MODEL), lambda b, n: (b, 0, 0)),
                  pl.BlockSpec((1, 1, 3 * D_MODEL), lambda b, n: (B, 0, 0)),
                  pl.BlockSpec((1, D_MODEL), lambda b, n: (0, 0)),
                  pl.BlockSpec((D_MODEL, INPROJ_TN), lambda b, n: (0, n))],
        out_specs=pl.BlockSpec((1, ROWS, INPROJ_TN), lambda b, n: (b, 0, n)),
        out_shape=jax.ShapeDtypeStruct((B, ROWS, P_COLS), BF16),
        scratch_shapes=[pltpu.VMEM((ROWS, D_MODEL), BF16)],
        compiler_params=_params(("arbitrary", "arbitrary")),
        name="inproj",
    )(x, cx, mod3, mod3, pre_g.reshape(1, -1), w_big)


def _mla_prep_kernel(p_ref, qn_ref, kvn_ref, wq_ref, wqx_ref, wk_ref, wv_ref, e1_ref, e2_ref,
                     cq_ref, sq_ref, ck_ref, sk_ref, q_ref, k_ref, v_ref):
    pm = p_ref[0]
    cq = pm[:, 0:MLA_Q_RANK].astype(F32)
    ckv = pm[:, MLA_Q_RANK:MLA_Q_RANK + MLA_KV_RANK].astype(F32)
    misc = pm[:, MLA_Q_RANK + MLA_KV_RANK:]
    cqn = (cq * lax.rsqrt(jnp.mean(cq * cq, axis=-1, keepdims=True) + EPS) * qn_ref[...]).astype(BF16)
    ckvn = (ckv * lax.rsqrt(jnp.mean(ckv * ckv, axis=-1, keepdims=True) + EPS) * kvn_ref[...]).astype(BF16)
    q = jnp.dot(cqn, wq_ref[...], preferred_element_type=F32)
    qx = jnp.dot(cqn, wqx_ref[...], preferred_element_type=F32)
    kn = jnp.dot(ckvn, wk_ref[...], preferred_element_type=F32)
    v = jnp.dot(ckvn, wv_ref[...], preferred_element_type=F32)
    kr = jnp.dot(misc, e1_ref[...], preferred_element_type=F32)
    krx = jnp.dot(misc, e2_ref[...], preferred_element_type=F32)
    krot = kr * ck_ref[...] + krx * sk_ref[...]
    cosq, sinq = cq_ref[...], sq_ref[...]
    for h in range(MLA_HEADS):
        sl = slice(h * MLA_SLOT, (h + 1) * MLA_SLOT)
        q_ref[0, :, sl] = (q[:, sl] * cosq + qx[:, sl] * sinq).astype(BF16)
        k_ref[0, :, sl] = (kn[:, sl] + krot).astype(BF16)
    lane = lax.broadcasted_iota(jnp.int32, (1, MLA_HEADS * MLA_SLOT), 1)
    upper = (lane % MLA_SLOT) >= MLA_V
    odd = (lane // MLA_SLOT) % 2 == 1
    v_ref[0] = jnp.where(upper != odd, 1.0, v).astype(BF16)


def _mla_prep(P, wts, n_tiles):
    B = P.shape[0]
    width = MLA_HEADS * MLA_SLOT
    cos_q, sin_q, cos_k, sin_k = (jnp.asarray(t) for t in _rope_tables())
    const = lambda b, t: (0, 0)
    tab = pl.BlockSpec((TILE, MLA_SLOT), lambda b, t: (t, 0))
    out = pl.BlockSpec((1, TILE, width), lambda b, t: (b, t, 0))
    shp = jax.ShapeDtypeStruct((B, ROWS, width), BF16)
    return pl.pallas_call(
        _mla_prep_kernel,
        grid=(B, N_TILES),
        in_specs=[pl.BlockSpec((1, TILE, 512), lambda b, t: (b, t, OFF_MLA // 512)),
                  pl.BlockSpec((1, MLA_Q_RANK), const),
                  pl.BlockSpec((1, MLA_KV_RANK), const),
                  pl.BlockSpec((MLA_Q_RANK, width), const),
                  pl.BlockSpec((MLA_Q_RANK, width), const),
                  pl.BlockSpec((MLA_KV_RANK, width), const),
                  pl.BlockSpec((MLA_KV_RANK, width), const),
                  pl.BlockSpec((128, MLA_SLOT), const),
                  pl.BlockSpec((128, MLA_SLOT), const),
                  tab, tab, tab, tab],
        out_specs=[out, out, out],
        out_shape=[shp, shp, shp],
        compiler_params=_params(("arbitrary", "arbitrary")),
        name="mla_prep",
    )(P, wts['q_norm'], wts['kv_norm'], wts['wq'], wts['wqx'], wts['wk'], wts['wv'],
      wts['e1'], wts['e2'], cos_q, sin_q, cos_k, sin_k)


def _mla_attn_kernel(q_ref, k_ref, v_ref, o_ref):
    lane = lax.broadcasted_iota(jnp.int32, (1, MLA_SLOT), 1)

    def attend(k0, nk):
        for pr in range(MLA_HEADS_PER_STEP // 2):
            acc = None
            for j in range(2):
                h = 2 * pr + j
                sl = slice(h * MLA_SLOT, (h + 1) * MLA_SLOT)
                q = q_ref[0, :, sl]
                k = k_ref[0, k0:k0 + nk, sl]
                v = v_ref[0, k0:k0 + nk, sl]
                s = lax.dot_general(q, k, (((1,), (1,)), ((), ())), preferred_element_type=F32)
                m = jnp.max(s, axis=-1, keepdims=True)
                p = jnp.exp2(s - m).astype(BF16)
                o = jnp.dot(p, v, preferred_element_type=F32)
                r = pltpu.roll(o, MLA_V, 1)
                keep = (lane >= MLA_V) if j else (lane < MLA_V)
                o = jnp.where(keep, o * (1.0 / r), 0.0)
                acc = o if acc is None else acc + o
            o_ref[0, :, pr * MLA_SLOT:(pr + 1) * MLA_SLOT] = acc.astype(BF16)

    is_ctx = pl.program_id(2) == N_LAT_TILES

    @pl.when(jnp.logical_not(is_ctx))
    def _():
        attend(0, ROWS)

    @pl.when(is_ctx)
    def _():
        attend(SEQ, CTX_LEN)


def _mla_attn(q, k, v, n_tiles):
    B = q.shape[0]
    width = MLA_HEADS_PER_STEP * MLA_SLOT
    return pl.pallas_call(
        _mla_attn_kernel,
        grid=(B, MLA_HEADS // MLA_HEADS_PER_STEP, n_tiles),
        in_specs=[pl.BlockSpec((1, TILE, width), lambda b, p, t: (b, t, p)),
                  pl.BlockSpec((1, ROWS, width), lambda b, p, t: (b, 0, p)),
                  pl.BlockSpec((1, ROWS, width), lambda b, p, t: (b, 0, p))],
        out_specs=pl.BlockSpec((1, TILE, width // 2), lambda b, p, t: (b, t, p)),
        out_shape=jax.ShapeDtypeStruct((B, n_tiles * TILE, MLA_HEADS * MLA_V), BF16),
        compiler_params=_params(("arbitrary", "arbitrary", "arbitrary")),
        name="mla_attn",
    )(q, k, v)


def _na_kernel(q_ref, k_ref, v_ref, bias_ref, o_ref):
    g = pl.program_id(1)
    lane = lax.broadcasted_iota(jnp.int32, (1, 2 * NA_HEAD_DIM), 1)
    scale = NA_HEAD_DIM ** -0.5
    nk_win = NA_KROWS * GRID_W

    def head_masks(h):
        return (lane >= NA_HEAD_DIM) if (h % 2) else (lane < NA_HEAD_DIM)

    def finish(parts, l):
        inv = 1.0 / l
        return sum(parts[1:], parts[0]) * inv

    @pl.when(g < N_LAT_TILES)
    def _():
        ks = jnp.clip(g * NA_GROUP - NA_WIN_R // 2, 0, GRID_ROWS - NA_KROWS)
        start = pl.multiple_of(ks * GRID_W, GRID_W)
        for p in range(NA_HEADS // 2):
            sl = slice(p * 128, (p + 1) * 128)
            qp = q_ref[0, :, sl] * scale
            kw = k_ref[0, pl.ds(start, nk_win), sl]
            vw = v_ref[0, pl.ds(start, nk_win), sl]
            kc = k_ref[0, SEQ:ROWS, sl]
            vc = v_ref[0, SEQ:ROWS, sl]
            acc = None
            for h in (2 * p, 2 * p + 1):
                mk = head_masks(h)
                qm = jnp.where(mk, qp, 0).astype(BF16)
                s_w = lax.dot_general(qm, kw, (((1,), (1,)), ((), ())),
                                      preferred_element_type=F32) + bias_ref[0, h]
                s_c = lax.dot_general(qm, kc, (((1,), (1,)), ((), ())), preferred_element_type=F32)
                m = jnp.maximum(jnp.max(s_w, axis=-1, keepdims=True), jnp.max(s_c, axis=-1, keepdims=True))
                p_w = jnp.exp(s_w - m)
                p_c = jnp.exp(s_c - m)
                l = jnp.sum(p_w, axis=-1, keepdims=True) + jnp.sum(p_c, axis=-1, keepdims=True)
                o = finish([jnp.dot(p_w.astype(BF16), jnp.where(mk, vw, 0).astype(BF16), preferred_element_type=F32),
                            jnp.dot(p_c.astype(BF16), jnp.where(mk, vc, 0).astype(BF16), preferred_element_type=F32)], l)
                acc = o if acc is None else acc + o
            o_ref[0, :, sl] = acc.astype(BF16)

    @pl.when(g == N_LAT_TILES)
    def _():
        for p in range(NA_HEADS // 2):
            sl = slice(p * 128, (p + 1) * 128)
            qp = q_ref[0, :, sl] * scale
            kc = k_ref[0, SEQ:ROWS, sl]
            vc = v_ref[0, SEQ:ROWS, sl]
            acc = None
            for h in (2 * p, 2 * p + 1):
                mk = head_masks(h)
                qm = jnp.where(mk, qp, 0).astype(BF16)
                s_c = lax.dot_general(qm, kc, (((1,), (1,)), ((), ())), preferred_element_type=F32)
                m = jnp.max(s_c, axis=-1, keepdims=True)
                p_c = jnp.exp(s_c - m)
                l = jnp.sum(p_c, axis=-1, keepdims=True)
                o = finish([jnp.dot(p_c.astype(BF16), jnp.where(mk, vc, 0).astype(BF16), preferred_element_type=F32)], l)
                acc = o if acc is None else acc + o
            o_ref[0, :, sl] = acc.astype(BF16)


def _na_attn(P, bias_tab, n_tiles):
    B = P.shape[0]
    width = NA_HEADS * NA_HEAD_DIM
    nq, nk = NA_GROUP * GRID_W, NA_KROWS * GRID_W

    def bias_map(b, g):
        return (jnp.where(g == 0, 0, jnp.where(g >= N_LAT_TILES - 1, 2, 1)), 0, 0, 0)

    return pl.pallas_call(
        _na_kernel,
        grid=(B, n_tiles),
        in_specs=[pl.BlockSpec((1, TILE, width), lambda b, g: (b, g, OFF_NA_Q // width)),
                  pl.BlockSpec((1, ROWS, width), lambda b, g: (b, 0, OFF_NA_K // width)),
                  pl.BlockSpec((1, ROWS, width), lambda b, g: (b, 0, OFF_NA_V // width)),
                  pl.BlockSpec((1, NA_HEADS, nq, nk), bias_map)],
        out_specs=pl.BlockSpec((1, TILE, width), lambda b, g: (b, g, 0)),
        out_shape=jax.ShapeDtypeStruct((B, n_tiles * TILE, width), BF16),
        compiler_params=_params(("arbitrary", "arbitrary")),
        name="na_attn",
    )(P, P, P, bias_tab)


def _na_bias_table(rpb):
    onehot, col_mask, drow_blocks = _na_bias_index()
    rc = jnp.dot(rpb.astype(F32).reshape(-1, 2 * NA_WIN_C - 1), jnp.asarray(onehot),
                 precision=lax.Precision.HIGHEST)
    rc = rc.reshape(NA_HEADS, 2 * NA_WIN_R - 1, GRID_W, GRID_W)
    rc = jnp.where(col_mask[None, None], rc, NA_NEG)
    neg = jnp.full((NA_HEADS, GRID_W, GRID_W), NA_NEG, F32)
    pats = []
    for p in range(3):
        rows = []
        for i in range(NA_GROUP):
            rows.append(jnp.concatenate(
                [neg if d < 0 else rc[:, d] for d in drow_blocks[p][i]], axis=-1))
        pats.append(jnp.concatenate(rows, axis=1))
    return jnp.stack(pats, axis=0)


def _hy_filter_kernel(z_ref, tn_ref, dl_ref, w1_ref, b1_ref, fr_ref, w2_ref, b2_ref, w3_ref, o_ref, a_ref):
    hp = lax.Precision.HIGHEST

    @pl.when(pl.program_id(0) == 0)
    def _():
        fr = fr_ref[...]
        a = jnp.sin(fr[0:1, :] * (jnp.dot(z_ref[...], w1_ref[...], preferred_element_type=F32, precision=hp)
                                  + b1_ref[...]))
        a_ref[...] = jnp.sin(fr[1:2, :] * (jnp.dot(a, w2_ref[...], preferred_element_type=F32, precision=hp)
                                           + b2_ref[...]))

    filt = jnp.dot(a_ref[...], w3_ref[...], preferred_element_type=F32, precision=hp)
    window = jnp.exp(-tn_ref[...] * dl_ref[...])
    f0 = filt[:, 0:HY_CT] * window
    f1 = filt[:, HY_CT:] * window
    den = (jnp.sum(jnp.abs(f0), axis=0, keepdims=True) + jnp.sum(jnp.abs(f1), axis=0, keepdims=True)) + EPS
    o_ref[:, 0:HY_CT] = f0 / den
    o_ref[:, HY_CT:] = f1 / den


def _hy_filters(L, w1p, b1, freq, w2, b2, w3):
    zp, t_norm, deltas = (jnp.asarray(t) for t in _hyena_pos_features(L))
    n_ct = HY_WIDTH // HY_CT
    w3t = w3.reshape(HY_FFN, 2, 2, n_ct, HY_CT).transpose(0, 1, 3, 2, 4).reshape(HY_FFN, 4 * HY_WIDTH)
    const = lambda n: (0, 0)
    return pl.pallas_call(
        _hy_filter_kernel,
        grid=(2 * n_ct,),
        in_specs=[pl.BlockSpec((L, 128), const), pl.BlockSpec((L, 1), const),
                  pl.BlockSpec((1, HY_CT), lambda n: (0, n % n_ct)), pl.BlockSpec((128, HY_FFN), const),
                  pl.BlockSpec((1, HY_FFN), const), pl.BlockSpec((2, HY_FFN), const),
                  pl.BlockSpec((HY_FFN, HY_FFN), const), pl.BlockSpec((1, HY_FFN), const),
                  pl.BlockSpec((HY_FFN, 2 * HY_CT), lambda n: (0, n))],
        out_specs=pl.BlockSpec((L, 2 * HY_CT), lambda n: (0, n)),
        out_shape=jax.ShapeDtypeStruct((L, 4 * HY_WIDTH), F32),
        scratch_shapes=[pltpu.VMEM((L, HY_FFN), F32)],
        compiler_params=_params(("arbitrary",)),
        name="hyena_filters",
    )(zp, t_norm, deltas, w1p, b1.reshape(1, -1), freq, w2, b2.reshape(1, -1), w3t)


def _hy_spectrum_kernel(f_ref, c2_ref, s2_ref, cp_ref, sp_ref, o_ref, *, L):
    half = f_ref.shape[1] // 2
    row = lax.broadcasted_iota(jnp.int32, (L, half), 0)
    ff = f_ref[:, 0:half]
    fb = jnp.where(row == 0, 0.0, f_ref[:, half:])

    fs = (ff + fb).astype(BF16)
    fd = (ff - fb).astype(BF16)
    a_s = jnp.dot(c2_ref[...], fs, preferred_element_type=F32)
    b_s = jnp.dot(s2_ref[...], fs, preferred_element_type=F32)
    a_d = jnp.dot(c2_ref[...], fd, preferred_element_type=F32)
    b_d = jnp.dot(s2_ref[...], fd, preferred_element_type=F32)
    cp, sp = cp_ref[...], sp_ref[...]
    scale = 1.0 / L
    o_ref[0] = (cp * a_s + sp * b_s) * scale
    o_ref[1] = (sp * a_d - cp * b_d) * scale


def _hy_spectrum(filt, L):
    c2, s2, cphi, sphi = _dft_tables(L)
    c2, s2 = jnp.asarray(c2, BF16), jnp.asarray(s2, BF16)
    n_ct = HY_WIDTH // HY_CT
    ft = min(L, 512)
    out = pl.pallas_call(
        functools.partial(_hy_spectrum_kernel, L=L),
        grid=(2 * n_ct, L // ft),
        in_specs=[pl.BlockSpec((L, 2 * HY_CT), lambda n, f: (0, n)),
                  pl.BlockSpec((ft, L), lambda n, f: (f, 0)),
                  pl.BlockSpec((ft, L), lambda n, f: (f, 0)),
                  pl.BlockSpec((ft, 1), lambda n, f: (f, 0)),
                  pl.BlockSpec((ft, 1), lambda n, f: (f, 0))],
        out_specs=pl.BlockSpec((2, ft, HY_CT), lambda n, f: (0, f, n)),
        out_shape=jax.ShapeDtypeStruct((2, L, 2 * n_ct * HY_CT), F32),
        compiler_params=_params(("arbitrary", "arbitrary")),
        name="hyena_spectrum",
    )(filt, c2, s2, jnp.asarray(cphi), jnp.asarray(sphi))
    return out, c2, s2


def _hy_conv_kernel(x1_ref, x2_ref, z_ref, w_ref, b_ref, skip_ref, h_ref, c2_ref, s2_ref, o_ref,
                    yre_ref, yim_ref, *, L):
    row = lax.broadcasted_iota(jnp.int32, (L, HY_CT), 0)

    def short_conv(p_ref, i):
        p = p_ref[0].astype(F32)
        w = w_ref[i]
        prev = jnp.where(row == 0, 0.0, pltpu.roll(p, 1, 0))
        nxt = jnp.where(row == L - 1, 0.0, pltpu.roll(p, L - 1, 0))
        return prev * w[0:1, :] + p * w[1:2, :] + nxt * w[2:3, :] + b_ref[i]

    ft = min(L, 512)
    z = short_conv(z_ref, 2)
    for n, gate_ref in enumerate((x1_ref, x2_ref)):
        zb = z.astype(BF16)
        for f in range(L // ft):
            fs = slice(f * ft, (f + 1) * ft)
            ur = jnp.dot(c2_ref[fs, :], zb, preferred_element_type=F32)
            us = jnp.dot(s2_ref[fs, :], zb, preferred_element_type=F32)
            hre, him = h_ref[0, n, fs, :], h_ref[1, n, fs, :]
            yre_ref[fs, :] = (ur * hre + us * him).astype(BF16)
            yim_ref[fs, :] = (ur * him - us * hre).astype(BF16)
        y = (jnp.dot(c2_ref[...], yre_ref[...], preferred_element_type=F32)
             - jnp.dot(s2_ref[...], yim_ref[...], preferred_element_type=F32))
        z = short_conv(gate_ref, n) * (y + z * skip_ref[n])
    o_ref[0] = z.astype(BF16)


def _hy_conv(P, row_blk, L, conv_w, conv_b, skip, spec, c2, s2):
    B = P.shape[0]
    n_ct = HY_WIDTH // HY_CT
    w4 = conv_w.reshape(3, 3, n_ct, HY_CT).transpose(2, 1, 0, 3)
    b4 = conv_b.reshape(3, n_ct, 1, HY_CT).transpose(1, 0, 2, 3)
    sk4 = skip.reshape(2, n_ct, 1, HY_CT).transpose(1, 0, 2, 3)
    h5 = spec.reshape(2, L, 2, n_ct, HY_CT).transpose(3, 0, 2, 1, 4)

    def pcol(off):
        return pl.BlockSpec((1, L, HY_CT), lambda c, b: (b, row_blk, off // HY_CT + c))

    const = lambda c, b: (0, 0)
    return pl.pallas_call(
        functools.partial(_hy_conv_kernel, L=L),
        grid=(n_ct, B),
        in_specs=[pcol(OFF_HY_X1), pcol(OFF_HY_X2), pcol(OFF_HY_ZIN),
                  pl.BlockSpec((None, 3, 3, HY_CT), lambda c, b: (c, 0, 0, 0)),
                  pl.BlockSpec((None, 3, 1, HY_CT), lambda c, b: (c, 0, 0, 0)),
                  pl.BlockSpec((None, 2, 1, HY_CT), lambda c, b: (c, 0, 0, 0)),
                  pl.BlockSpec((None, 2, 2, L, HY_CT), lambda c, b: (c, 0, 0, 0, 0),
                               pipeline_mode=pl.Buffered(1)),
                  _single((L, L), const), _single((L, L), const)],
        out_specs=pl.BlockSpec((1, L, HY_CT), lambda c, b: (b, 0, c)),
        out_shape=jax.ShapeDtypeStruct((B, L, HY_WIDTH), BF16),
        scratch_shapes=[pltpu.VMEM((L, HY_CT), BF16), pltpu.VMEM((L, HY_CT), BF16)],
        compiler_params=_params(("arbitrary", "arbitrary")),
        name="hyena_conv",
    )(P, P, P, w4, b4, sk4, h5, c2, s2)


def _gla_kernel(qk_ref, v_ref, misc_ref, wg_ref, bg_ref, ng_ref, o_ref,
                qe_s, ke_s, qi_s, ko_s, dec_s, o_scr, s_scr):
    C = GLA_CHUNK
    HK = GLA_HEADS * GLA_DK
    NC = GLA_NCHUNK
    logit = jnp.dot(misc_ref[0], wg_ref[...], preferred_element_type=F32) + bg_ref[...]
    g = (jnp.minimum(logit, 0.0) - jnp.log(1.0 + jnp.exp(-jnp.abs(logit)))) * (1.0 / GLA_TAU)
    rmod = lax.broadcasted_iota(jnp.int32, (ROWS, 1), 0) % C
    p = g
    shift = 1
    while shift < C:
        p = p + jnp.where(rmod >= shift, pltpu.roll(p, shift, 0), 0.0)
        shift *= 2
    p3 = p.reshape(NC, C, 2 * HK)
    g3 = g.reshape(NC, C, 2 * HK)
    tot = p3[:, C - 1:C, :]
    q3 = (qk_ref[0, :, 0:HK].astype(F32) * (GLA_DK ** -0.5)).reshape(NC, C, HK)
    k3 = qk_ref[0, :, HK:2 * HK].astype(F32).reshape(NC, C, HK)
    for d in range(2):
        cs = slice(d * HK, (d + 1) * HK)
        if d == 0:
            bc = p3[:, :, cs]
            bm = bc[:, C // 2 - 1:C // 2, :]
        else:
            bc = tot[:, :, cs] - p3[:, :, cs] + g3[:, :, cs]
            bm = bc[:, C // 2:C // 2 + 1, :]
        bl = tot[:, :, cs]
        qe = q3 * jnp.exp(bc - bm)
        ke = k3 * jnp.exp(bm - bc)
        qe_s[d] = qe.reshape(ROWS, HK).astype(BF16)
        ke_s[d] = ke.reshape(ROWS, HK).astype(BF16)
        qi_s[d] = (qe * jnp.exp(bm)).reshape(ROWS, HK).astype(BF16)
        ko_s[d] = (ke * jnp.exp(bl - bm)).reshape(ROWS, HK).astype(BF16)
        dec_s[d] = jnp.broadcast_to(jnp.exp(bl), (NC, 8, HK)).reshape(NC * 8, HK)
    s_scr[...] = jnp.zeros_like(s_scr)

    ri = lax.broadcasted_iota(jnp.int32, (GLA_HEADS * C, C), 0) % C
    ci = lax.broadcasted_iota(jnp.int32, (GLA_HEADS * C, C), 1)
    keep = (ri >= ci, ri <= ci)
    lane = lax.broadcasted_iota(jnp.int32, (1, HK), 1)
    lane2 = lax.broadcasted_iota(jnp.int32, (1, 2 * GLA_DK), 1)

    def one_chunk(d, c):
        rows = pl.ds(pl.multiple_of(c * C, C), C)
        qe = qe_s[d, rows, :]
        ke = ke_s[d, rows, :]
        qi = qi_s[d, rows, :]
        ko = ko_s[d, rows, :]
        dec = dec_s[d, pl.ds(pl.multiple_of(c * 8, 8), 8), :][0:1, :]
        stack = jnp.concatenate(
            [jnp.where((lane >= h * GLA_DK) & (lane < (h + 1) * GLA_DK), qe, 0).astype(BF16)
             for h in range(GLA_HEADS)], axis=0)
        a_all = lax.dot_general(stack, ke, (((1,), (1,)), ((), ())), preferred_element_type=F32)
        a_all = jnp.where(keep[d], a_all, 0.0).astype(BF16)
        for h in range(GLA_HEADS):
            sl = slice((h // 2) * 128, (h // 2 + 1) * 128)
            mk = (lane2 >= GLA_DK) if (h % 2) else (lane2 < GLA_DK)
            v_h = v_ref[0, rows, h * GLA_DV:(h + 1) * GLA_DV]
            st = s_scr[d, h]
            qi_h = jnp.where(mk, qi[:, sl], 0).astype(BF16)
            o = (jnp.dot(a_all[h * C:(h + 1) * C], v_h, preferred_element_type=F32)
                 + lax.dot_general(qi_h, st.astype(BF16), (((1,), (1,)), ((), ())),
                                   preferred_element_type=F32))
            ko_h = jnp.where(mk, ko[:, sl], 0).astype(BF16)
            kv_t = lax.dot_general(v_h, ko_h, (((0,), (0,)), ((), ())), preferred_element_type=F32)
            s_scr[d, h] = st * dec[:, sl] + kv_t
            o_scr[d, rows, h * GLA_DV:(h + 1) * GLA_DV] = o

    def step(i, carry):
        cf = jnp.where(i < GLA_CTX_CHUNKS, GLA_NCHUNK - GLA_CTX_CHUNKS + i, i - GLA_CTX_CHUNKS)
        cb = GLA_NCHUNK - 1 - i
        one_chunk(0, cf)
        one_chunk(1, cb)
        return carry

    lax.fori_loop(0, GLA_NCHUNK, step, 0, unroll=2)

    ng = ng_ref[...]
    for t in range(N_TILES):
        rs = slice(t * TILE, (t + 1) * TILE)
        for h in range(GLA_HEADS):
            cs = slice(h * GLA_DV, (h + 1) * GLA_DV)
            o = o_scr[0, rs, cs] + o_scr[1, rs, cs]
            o = o * lax.rsqrt(jnp.mean(o * o, axis=-1, keepdims=True) + EPS) * ng
            o_ref[0, rs, cs] = o.astype(BF16)


def _gla(P, wg, bg, norm_g):
    B = P.shape[0]
    width = GLA_HEADS * GLA_DV
    const = lambda b: (0, 0)
    return pl.pallas_call(
        _gla_kernel,
        grid=(B,),
        in_specs=[pl.BlockSpec((1, ROWS, 512), lambda b: (b, 0, OFF_GLA_QK // 512)),
                  pl.BlockSpec((1, ROWS, width), lambda b: (b, 0, OFF_GLA_V // width)),
                  pl.BlockSpec((1, ROWS, 128), lambda b: (b, 0, (OFF_MLA + 384) // 128)),
                  pl.BlockSpec((128, 2 * GLA_HEADS * GLA_DK), const),
                  pl.BlockSpec((1, 2 * GLA_HEADS * GLA_DK), const),
                  pl.BlockSpec((1, GLA_DV), const)],
        out_specs=pl.BlockSpec((1, ROWS, width), lambda b: (b, 0, 0)),
        out_shape=jax.ShapeDtypeStruct((B, ROWS, width), BF16),
        scratch_shapes=[pltpu.VMEM((2, ROWS, GLA_HEADS * GLA_DK), BF16),
                        pltpu.VMEM((2, ROWS, GLA_HEADS * GLA_DK), BF16),
                        pltpu.VMEM((2, ROWS, GLA_HEADS * GLA_DK), BF16),
                        pltpu.VMEM((2, ROWS, GLA_HEADS * GLA_DK), BF16),
                        pltpu.VMEM((2, GLA_NCHUNK * 8, GLA_HEADS * GLA_DK), F32),
                        pltpu.VMEM((2, ROWS, width), F32),
                        pltpu.VMEM((2, GLA_HEADS, GLA_DV, 2 * GLA_DK), F32)],
        compiler_params=_params(("arbitrary",)),
        name="gla",
    )(P, P, P, wg, bg, norm_g.reshape(1, -1))


def _merge_kernel(ya_ref, yb_ref, yh_ref, yg_ref, za_ref, zb_ref, zh_ref, zg_ref, gt_ref,
                  x_ref, mod_ref, pg_ref, wb_ref, wo_ref, o_ref):
    merged = None
    one = jnp.asarray(1.0, BF16)
    for i, (y_ref, z_ref) in enumerate(((ya_ref, za_ref), (yb_ref, zb_ref), (yh_ref, zh_ref), (yg_ref, zg_ref))):
        z = z_ref[0]
        t = y_ref[0] * (z * (jnp.tanh(z) + one))
        br = jnp.dot(t, wb_ref[i], preferred_element_type=F32)
        gl = gt_ref[0, :, i * D_MODEL:(i + 1) * D_MODEL].astype(F32)
        term = br * (jnp.tanh(gl) + 1.0)
        merged = term if merged is None else merged + term
    out = jnp.dot(merged.astype(BF16), wo_ref[...], preferred_element_type=F32)
    xn = out * lax.rsqrt(jnp.mean(out * out, axis=-1, keepdims=True) + EPS) * pg_ref[...]
    gate = mod_ref[0][:, 2 * D_MODEL:3 * D_MODEL]
    o_ref[0] = x_ref[0] + gate * xn


def _merge(x, P, ya, yb, yh, yg, mod3, mod_row, post_g, wb, wo, row_off):
    B, R, _ = x.shape
    n_t = R // TILE
    const2 = lambda b, t: (0, 0)

    def uni(width, off):
        return pl.BlockSpec((1, TILE, width), lambda b, t: (b, t + row_off, off // width))

    return pl.pallas_call(
        _merge_kernel,
        grid=(B, n_t),
        in_specs=[uni(512, 0), uni(512, 0),
                  pl.BlockSpec((1, TILE, 512), lambda b, t: (b, t, 0)),
                  uni(512, 0),
                  uni(512, OFF_MLA_Z), uni(512, OFF_NA_Z), uni(512, OFF_HY_Z), uni(512, OFF_GLA_Z),
                  uni(4 * D_MODEL, OFF_MERGE),
                  pl.BlockSpec((1, TILE, D_MODEL), lambda b, t: (b, t, 0)),
                  pl.BlockSpec((1, 1, 3 * D_MODEL), lambda b, t: (mod_row(b), 0, 0)),
                  pl.BlockSpec((1, D_MODEL), const2),
                  pl.BlockSpec((4, 512, D_MODEL), lambda b, t: (0, 0, 0)),
                  pl.BlockSpec((D_MODEL, D_MODEL), const2)],
        out_specs=pl.BlockSpec((1, TILE, D_MODEL), lambda b, t: (b, t, 0)),
        out_shape=jax.ShapeDtypeStruct((B, R, D_MODEL), F32),
        compiler_params=_params(("arbitrary", "arbitrary")),
        name="merge",
    )(ya, yb, yh, yg, P, P, P, P, P, x, mod3, post_g.reshape(1, -1), wb, wo)


def _build_w_big(w):
    def cols(name):
        a, b = _ref_cols(name)
        return w[:, a:b]

    kr = cols('mla_kr')
    krx = kr[:, _ROPE_PERM] * _ROPE_SIGN
    pad32 = jnp.zeros((D_MODEL, 32), w.dtype)
    parts = [0.5 * cols('merge'), cols('mla_cq'), cols('mla_ckv'), kr, cols('gla_glr'), krx, pad32,
             0.5 * cols('mla_z'), cols('na_qkv'), 0.5 * cols('na_z'), cols('hy_proj'), 0.5 * cols('hy_z'),
             cols('gla_qk'), cols('gla_v'), 0.5 * cols('gla_z')]
    big = jnp.concatenate(parts, axis=1).astype(BF16)
    assert big.shape == (D_MODEL, P_COLS)
    return big


def _build_mla_weights(q_norm, w_uq, kv_norm, w_ukv):
    dq = MLA_NOPE + MLA_ROPE
    wq3 = w_uq.reshape(MLA_Q_RANK, MLA_HEADS, dq)
    rot = wq3[:, :, MLA_NOPE:]
    rotx = rot[:, :, _ROPE_PERM] * _ROPE_SIGN
    zq = jnp.zeros((MLA_Q_RANK, MLA_HEADS, MLA_SLOT - dq), F32)
    wq = jnp.concatenate([wq3, zq], axis=-1).reshape(MLA_Q_RANK, MLA_HEADS * MLA_SLOT)
    wqx = jnp.concatenate([jnp.zeros((MLA_Q_RANK, MLA_HEADS, MLA_NOPE), F32), rotx, zq],
                          axis=-1).reshape(MLA_Q_RANK, MLA_HEADS * MLA_SLOT)
    wkv3 = w_ukv.reshape(MLA_KV_RANK, MLA_HEADS, MLA_NOPE + MLA_V)
    zk = jnp.zeros((MLA_KV_RANK, MLA_HEADS, MLA_SLOT - MLA_NOPE), F32)
    wk = jnp.concatenate([wkv3[:, :, :MLA_NOPE], zk], axis=-1).reshape(MLA_KV_RANK, MLA_HEADS * MLA_SLOT)
    wv3 = wkv3[:, :, MLA_NOPE:]
    zv = jnp.zeros_like(wv3)
    even = jnp.concatenate([wv3, zv], axis=-1)
    odd = jnp.concatenate([zv, wv3], axis=-1)
    sel = (np.arange(MLA_HEADS) % 2 == 1)[None, :, None]
    wv = jnp.where(sel, odd, even).reshape(MLA_KV_RANK, MLA_HEADS * MLA_SLOT)
    e1 = np.zeros((128, MLA_SLOT), np.float32)
    e2 = np.zeros((128, MLA_SLOT), np.float32)
    for i in range(MLA_ROPE):
        e1[i, MLA_NOPE + i] = 1.0
        e2[64 + i, MLA_NOPE + i] = 1.0
    return dict(q_norm=q_norm.reshape(1, -1), kv_norm=kv_norm.reshape(1, -1),
                wq=wq.astype(BF16), wqx=wqx.astype(BF16), wk=wk.astype(BF16), wv=wv.astype(BF16),
                e1=jnp.asarray(e1, BF16), e2=jnp.asarray(e2, BF16))


def _build_gla_gate_weights(wg2, bg):
    hk = GLA_HEADS * GLA_DK
    wg = jnp.zeros((128, 2 * hk), F32)
    for i in range(2):
        wg = wg.at[32 + i * GLA_RANK:32 + (i + 1) * GLA_RANK, i * hk:(i + 1) * hk].set(wg2[i])
    return wg.astype(BF16), bg.reshape(1, 2 * hk)


def kernel(x, c, ctx, c_ctx, ada_w, ada_b, pre_g, post_g, w_in, mla_q_norm, mla_w_uq, mla_kv_norm, mla_w_ukv, na_rpb, hy_conv_w, hy_conv_b, hy_pe_w1, hy_pe_b1, hy_pe_freq, hy_pe_w2, hy_pe_b2, hy_pe_w3, hy_skip, gla_wg2, gla_bg, gla_norm, w_branch, w_out):
    B = x.shape[0]
    depth = ada_w.shape[0]
    mod_rows = -(-(B + 1) // 8) * 8
    cc = jnp.zeros((mod_rows, D_MODEL), F32).at[:B].set(c).at[B].set(c_ctx)
    cx = ctx
    for l in range(depth):
        need_ctx = l < depth - 1
        n_tiles = N_TILES if need_ctx else N_LAT_TILES
        mod3 = _adaln(cc, ada_w[l], ada_b[l]).reshape(mod_rows, 1, 3 * D_MODEL)
        P = _inproj(x, cx, mod3, pre_g[l], _build_w_big(w_in[l]))

        mla_w = _build_mla_weights(mla_q_norm[l], mla_w_uq[l], mla_kv_norm[l], mla_w_ukv[l])
        q, k, v = _mla_prep(P, mla_w, N_TILES)
        ya = _mla_attn(q, k, v, n_tiles)

        yb = _na_attn(P, _na_bias_table(na_rpb[l]), n_tiles)

        w1p = jnp.zeros((128, HY_FFN), F32).at[:HY_EMB].set(hy_pe_w1[l])

        def hyena(L, row_blk):
            filt = _hy_filters(L, w1p, hy_pe_b1[l], hy_pe_freq[l], hy_pe_w2[l], hy_pe_b2[l], hy_pe_w3[l])
            spec, c2, s2 = _hy_spectrum(filt, L)
            return _hy_conv(P, row_blk, L, hy_conv_w[l], hy_conv_b[l], hy_skip[l], spec, c2, s2)

        yh = hyena(SEQ, 0)

        wg, bgp = _build_gla_gate_weights(gla_wg2[l], gla_bg[l])
        yg = _gla(P, wg, bgp, gla_norm[l])

        wb = (0.5 * w_branch[l]).astype(BF16)
        wo = w_out[l].astype(BF16)
        if need_ctx:
            yhc = hyena(CTX_LEN, SEQ // CTX_LEN)
            cx = _merge(cx, P, ya, yb, yhc, yg, mod3, lambda b: B, post_g[l], wb, wo, N_LAT_TILES)
        x = _merge(x, P, ya, yb, yh, yg, mod3, lambda b: b, post_g[l], wb, wo, 0)
    return x
```

```python
import functools
import math

import numpy as np
import jax
import jax.numpy as jnp
from jax import lax
from jax.experimental import pallas as pl
from jax.experimental.pallas import tpu as pltpu

F32 = jnp.float32
BF16 = jnp.bfloat16

D_MODEL = 1024
SEQ = 2048
CTX_LEN = 256
ROWS = SEQ + CTX_LEN
GRID_W = 64
GRID_ROWS = SEQ // GRID_W
EPS = 1e-6
ROPE_BASE = 10000.0
TILE = 256
N_LAT_TILES = SEQ // TILE
N_TILES = ROWS // TILE

MLA_HEADS = 8
MLA_NOPE = 64
MLA_ROPE = 32
MLA_V = 64
MLA_Q_RANK = 256
MLA_KV_RANK = 128
MLA_SLOT = 128
MLA_HEADS_PER_STEP = 4
MLA_LOOKAHEAD = 2

NA_HEADS = 8
NA_HEAD_DIM = 64
NA_WIN_R = 8
NA_WIN_C = 16
NA_GROUP = 4
NA_KROWS = NA_GROUP + NA_WIN_R - 1
NA_NEG = -1e30

HY_WIDTH = 512
HY_EMB = 33
HY_FFN = 64
HY_TARGET = 1e-2
HY_FAST_DECAY_PCT = 0.3
HY_SLOW_DECAY_PCT = 1.5
HY_CT = 256

GLA_HEADS = 4
GLA_DK = 64
GLA_DV = 128
GLA_RANK = 16
GLA_TAU = 16.0
GLA_CHUNK = 64
GLA_NCHUNK = ROWS // GLA_CHUNK
GLA_CTX_CHUNKS = CTX_LEN // GLA_CHUNK

OFF_MERGE = 0
OFF_MLA = 4096
OFF_MLA_Z = 4608
OFF_NA_Q = 5120
OFF_NA_K = 5632
OFF_NA_V = 6144
OFF_NA_Z = 6656
OFF_HY_X1 = 7168
OFF_HY_X2 = 7680
OFF_HY_ZIN = 8192
OFF_HY_Z = 8704
OFF_GLA_QK = 9216
OFF_GLA_V = 9728
OFF_GLA_Z = 10240
P_COLS = 10752
INPROJ_TN = 512

VMEM_LIMIT = 56 * 1024 * 1024

_REF_LAYOUT = (
    ('mla_cq', 256), ('mla_ckv', 128), ('mla_kr', 32), ('mla_z', 512), ('na_qkv', 1536),
    ('na_z', 512), ('hy_proj', 1536), ('hy_z', 512), ('gla_qk', 512), ('gla_v', 512),
    ('gla_glr', 32), ('gla_z', 512), ('merge', 4096),
)


def _ref_cols(name):
    start = 0
    for n, w in _REF_LAYOUT:
        if n == name:
            return start, start + w
        start += w
    raise KeyError(name)


def _params(sem, vmem=VMEM_LIMIT):
    return pltpu.CompilerParams(dimension_semantics=sem, vmem_limit_bytes=vmem)


def _single(shape, index_map):
    return pl.BlockSpec(shape, index_map, pipeline_mode=pl.Buffered(1))


_ROPE_PERM = np.concatenate([np.arange(8, 16), np.arange(0, 8), np.arange(24, 32), np.arange(16, 24)])
_ROPE_SIGN = np.concatenate([-np.ones(8), np.ones(8), -np.ones(8), np.ones(8)]).astype(np.float32)


@functools.lru_cache(maxsize=None)
def _rope_tables():
    t = np.arange(SEQ)
    row = (t // GRID_W).astype(np.float32)
    col = (t % GRID_W).astype(np.float32)
    half = MLA_ROPE // 2
    inv = np.power(np.float32(ROPE_BASE), -np.arange(0, half, 2, dtype=np.float32) / np.float32(half))
    ar, ac = row[:, None] * inv, col[:, None] * inv
    ang = np.concatenate([ar, ar, ac, ac], axis=-1).astype(np.float32)
    cos = np.ones((ROWS, MLA_ROPE), np.float32)
    sin = np.zeros((ROWS, MLA_ROPE), np.float32)
    cos[:SEQ] = np.cos(ang)
    sin[:SEQ] = np.sin(ang)
    scale = np.float32((MLA_NOPE + MLA_ROPE) ** -0.5 * math.log2(math.e))
    cos_q = np.zeros((ROWS, MLA_SLOT), np.float32)
    sin_q = np.zeros((ROWS, MLA_SLOT), np.float32)
    cos_q[:, :MLA_NOPE] = scale
    cos_q[:, MLA_NOPE:MLA_NOPE + MLA_ROPE] = cos * scale
    sin_q[:, MLA_NOPE:MLA_NOPE + MLA_ROPE] = sin * scale
    cos_k = np.zeros((ROWS, MLA_SLOT), np.float32)
    sin_k = np.zeros((ROWS, MLA_SLOT), np.float32)
    cos_k[:, MLA_NOPE:MLA_NOPE + MLA_ROPE] = cos
    sin_k[:, MLA_NOPE:MLA_NOPE + MLA_ROPE] = sin
    return cos_q, sin_q, cos_k, sin_k


@functools.lru_cache(maxsize=None)
def _dft_tables(L):
    n = 2 * L
    k = np.arange(L, dtype=np.int64)
    prod = np.mod(np.outer(2 * k + 1, 2 * k + 1), 4 * n)
    ang = prod.astype(np.float64) * (math.pi / (2 * n))
    c2 = np.cos(ang).astype(np.float32)
    s2 = np.sin(ang).astype(np.float32)
    phi = (k.astype(np.float64) + 0.5) * (math.pi / n)
    cphi = np.cos(phi).astype(np.float32)[:, None]
    sphi = np.sin(phi).astype(np.float32)[:, None]
    return c2, s2, cphi, sphi


@functools.lru_cache(maxsize=None)
def _hyena_pos_features(L):
    f32 = np.float32
    t = np.arange(L, dtype=f32)
    t_norm = t / f32(max(L - 1, 1))
    bands = (HY_EMB - 1) // 2
    fr = np.linspace(1e-4, bands - 1, bands, dtype=f32)
    ang = (f32(2.0 * math.pi / L) * t[:, None] * fr[None, :]).astype(f32)
    z = np.concatenate([t_norm[:, None], np.cos(ang), -np.sin(ang)], axis=-1).astype(f32)
    zp = np.zeros((L, 128), f32)
    zp[:, :HY_EMB] = z
    max_decay = math.log(HY_TARGET) / HY_FAST_DECAY_PCT
    min_decay = math.log(HY_TARGET) / HY_SLOW_DECAY_PCT
    deltas = np.abs(np.linspace(min_decay, max_decay, HY_WIDTH, dtype=f32)).astype(f32)
    return zp, t_norm[:, None].astype(f32), deltas[None, :]


@functools.lru_cache(maxsize=None)
def _na_bias_index():
    wr = NA_WIN_R
    j = np.arange(GRID_W)
    c0 = np.clip(j - NA_WIN_C // 2, 0, GRID_W - NA_WIN_C)
    col_mask = (j[None, :] >= c0[:, None]) & (j[None, :] < c0[:, None] + NA_WIN_C)
    dcol = np.clip(j[None, :] - j[:, None], -(NA_WIN_C - 1), NA_WIN_C - 1) + NA_WIN_C - 1
    onehot = np.zeros((2 * NA_WIN_C - 1, GRID_W * GRID_W), np.float32)
    onehot[dcol.reshape(-1), np.arange(GRID_W * GRID_W)] = 1.0
    blocks = []
    for ra in (0, NA_GROUP, GRID_ROWS - NA_GROUP):
        ks = int(np.clip(ra - wr // 2, 0, GRID_ROWS - NA_KROWS))
        per_row = []
        for i in range(NA_GROUP):
            r = ra + i
            r0 = int(np.clip(r - wr // 2, 0, GRID_ROWS - wr))
            per_row.append(tuple((ks + jj - r + wr - 1) if (r0 <= ks + jj < r0 + wr) else -1
                                 for jj in range(NA_KROWS)))
        blocks.append(tuple(per_row))
    return onehot, col_mask, tuple(blocks)


def _adaln_kernel(c_ref, w_ref, b_ref, o_ref):
    c = c_ref[...]
    a = c * (1.0 / (1.0 + jnp.exp(-c)))
    o_ref[...] = jnp.dot(a, w_ref[...], preferred_element_type=F32,
                         precision=lax.Precision.HIGHEST) + b_ref[...]


def _adaln(cc, w, b):
    rows = cc.shape[0]
    tn = 768
    return pl.pallas_call(
        _adaln_kernel,
        grid=(3 * D_MODEL // tn,),
        in_specs=[pl.BlockSpec((rows, D_MODEL), lambda n: (0, 0)),
                  pl.BlockSpec((D_MODEL, tn), lambda n: (0, n)),
                  pl.BlockSpec((1, tn), lambda n: (0, n))],
        out_specs=pl.BlockSpec((rows, tn), lambda n: (0, n)),
        out_shape=jax.ShapeDtypeStruct((rows, 3 * D_MODEL), F32),
        compiler_params=_params(("arbitrary",)),
        name="adaln",
    )(cc, w, b.reshape(1, -1))


def _inproj_kernel(x_ref, cx_ref, modl_ref, modc_ref, g_ref, w_ref, o_ref, h_ref):
    @pl.when(pl.program_id(1) == 0)
    def _():
        g = g_ref[...]

        def norm_mod(xv, mod):
            ms = jnp.mean(xv * xv, axis=-1, keepdims=True)
            xn = xv * lax.rsqrt(ms + EPS) * g
            return (xn * (1.0 + mod[:, D_MODEL:2 * D_MODEL]) + mod[:, 0:D_MODEL]).astype(BF16)

        modl = modl_ref[0]
        for i in range(N_LAT_TILES):
            h_ref[i * TILE:(i + 1) * TILE, :] = norm_mod(x_ref[0, i * TILE:(i + 1) * TILE, :], modl)
        h_ref[SEQ:ROWS, :] = norm_mod(cx_ref[0], modc_ref[0])

    o_ref[0] = jnp.dot(h_ref[...], w_ref[...], preferred_element_type=F32).astype(BF16)


def _inproj(x, cx, mod3, pre_g, w_big):
    B = x.shape[0]
    return pl.pallas_call(
        _inproj_kernel,
        grid=(B, P_COLS // INPROJ_TN),
        in_specs=[pl.BlockSpec((1, SEQ, D_MODEL), lambda b, n: (b, 0, 0)),
                  pl.BlockSpec((1, CTX_LEN, D_MODEL), lambda b, n: (b, 0, 0)),
                  pl.BlockSpec((1, 1, 3 * D_MODEL), lambda b, n: (b, 0, 0)),
                  pl.BlockSpec((1, 1, 3 * D_MODEL), lambda b, n: (B, 0, 0)),
                  pl.BlockSpec((1, D_MODEL), lambda b, n: (0, 0)),
                  pl.BlockSpec((D_MODEL, INPROJ_TN), lambda b, n: (0, n))],
        out_specs=pl.BlockSpec((1, ROWS, INPROJ_TN), lambda b, n: (b, 0, n)),
        out_shape=jax.ShapeDtypeStruct((B, ROWS, P_COLS), BF16),
        scratch_shapes=[pltpu.VMEM((ROWS, D_MODEL), BF16)],
        compiler_params=_params(("arbitrary", "arbitrary")),
        name="inproj",
    )(x, cx, mod3, mod3, pre_g.reshape(1, -1), w_big)


def _mla_prep_kernel(p_ref, qn_ref, kvn_ref, wq_ref, wk_ref, wv_ref, e1_ref,
                     cq_ref, sq_ref, ck_ref, sk_ref, q_ref, k_ref, v_ref):
    pm = p_ref[0]
    cq = pm[:, 0:MLA_Q_RANK].astype(F32)
    ckv = pm[:, MLA_Q_RANK:MLA_Q_RANK + MLA_KV_RANK].astype(F32)
    misc = pm[:, MLA_Q_RANK + MLA_KV_RANK:]
    cqn = (cq * lax.rsqrt(jnp.mean(cq * cq, axis=-1, keepdims=True) + EPS) * qn_ref[...]).astype(BF16)
    ckvn = (ckv * lax.rsqrt(jnp.mean(ckv * ckv, axis=-1, keepdims=True) + EPS) * kvn_ref[...]).astype(BF16)
    q = jnp.dot(cqn, wq_ref[...], preferred_element_type=F32)
    qx = pltpu.roll(q, MLA_HEADS * MLA_SLOT - MLA_ROPE, 1)
    kn = jnp.dot(ckvn, wk_ref[...], preferred_element_type=F32)
    v = jnp.dot(ckvn, wv_ref[...], preferred_element_type=F32)
    kr = jnp.dot(misc, e1_ref[...], preferred_element_type=F32)
    krot = kr * ck_ref[...] + pltpu.roll(kr, MLA_SLOT - MLA_ROPE, 1) * sk_ref[...]
    cosq, sinq = cq_ref[...], sq_ref[...]
    for h in range(MLA_HEADS):
        sl = slice(h * MLA_SLOT, (h + 1) * MLA_SLOT)
        q_ref[0, :, sl] = (q[:, sl] * cosq + qx[:, sl] * sinq).astype(BF16)
        k_ref[0, :, sl] = (kn[:, sl] + krot).astype(BF16)
    lane = lax.broadcasted_iota(jnp.int32, (1, MLA_HEADS * MLA_SLOT), 1)
    upper = (lane % MLA_SLOT) >= MLA_V
    odd = (lane // MLA_SLOT) % 2 == 1
    v_ref[0] = jnp.where(upper != odd, 1.0, v).astype(BF16)


def _mla_prep(P, wts, n_tiles):
    B = P.shape[0]
    width = MLA_HEADS * MLA_SLOT
    cos_q, sin_q, cos_k, sin_k = (jnp.asarray(t) for t in _rope_tables())
    const = lambda b, t: (0, 0)
    tab = pl.BlockSpec((TILE, MLA_SLOT), lambda b, t: (t, 0))
    out = pl.BlockSpec((1, TILE, width), lambda b, t: (b, t, 0))
    shp = jax.ShapeDtypeStruct((B, ROWS, width), BF16)
    return pl.pallas_call(
        _mla_prep_kernel,
        grid=(B, N_TILES),
        in_specs=[pl.BlockSpec((1, TILE, 512), lambda b, t: (b, t, OFF_MLA // 512)),
                  pl.BlockSpec((1, MLA_Q_RANK), const),
                  pl.BlockSpec((1, MLA_KV_RANK), const),
                  pl.BlockSpec((MLA_Q_RANK, width), const),
                  pl.BlockSpec((MLA_KV_RANK, width), const),
                  pl.BlockSpec((MLA_KV_RANK, width), const),
                  pl.BlockSpec((128, MLA_SLOT), const),
                  tab, tab, tab, tab],
        out_specs=[out, out, out],
        out_shape=[shp, shp, shp],
        compiler_params=_params(("arbitrary", "arbitrary")),
        name="mla_prep",
    )(P, wts['q_norm'], wts['kv_norm'], wts['wq'], wts['wk'], wts['wv'],
      wts['e1'], cos_q, sin_q, cos_k, sin_k)


def _mla_attn_kernel(q_ref, k_ref, v_ref, o_ref):
    lane = lax.broadcasted_iota(jnp.int32, (1, MLA_SLOT), 1)

    def attend(k0, nk):
        def scores(h):
            sl = slice(h * MLA_SLOT, (h + 1) * MLA_SLOT)
            return lax.dot_general(q_ref[0, :, sl], k_ref[0, k0:k0 + nk, sl], (((1,), (1,)), ((), ())),
                                   preferred_element_type=F32)

        def head_out(h, s):
            sl = slice(h * MLA_SLOT, (h + 1) * MLA_SLOT)
            m = jnp.max(s, axis=-1, keepdims=True)
            p = jnp.exp2(s - m).astype(BF16)
            o = jnp.dot(p, v_ref[0, k0:k0 + nk, sl], preferred_element_type=F32)
            r = pltpu.roll(o, MLA_V, 1)
            keep = (lane >= MLA_V) if (h % 2) else (lane < MLA_V)
            return jnp.where(keep, o * (1.0 / r), 0.0)

        nh = MLA_HEADS_PER_STEP
        s = [scores(h) if h < MLA_LOOKAHEAD else None for h in range(nh)]
        outs = [None] * nh
        for h in range(nh):
            if h + MLA_LOOKAHEAD < nh:
                s[h + MLA_LOOKAHEAD] = scores(h + MLA_LOOKAHEAD)
            outs[h] = head_out(h, s[h])
            s[h] = None
            if h % 2 == 1:
                pr = h // 2
                o_ref[0, :, pr * MLA_SLOT:(pr + 1) * MLA_SLOT] = (outs[h - 1] + outs[h]).astype(BF16)

    is_ctx = pl.program_id(2) == N_LAT_TILES

    @pl.when(jnp.logical_not(is_ctx))
    def _():
        attend(0, ROWS)

    @pl.when(is_ctx)
    def _():
        attend(SEQ, CTX_LEN)


def _mla_attn(q, k, v, n_tiles):
    B = q.shape[0]
    width = MLA_HEADS_PER_STEP * MLA_SLOT
    return pl.pallas_call(
        _mla_attn_kernel,
        grid=(B, MLA_HEADS // MLA_HEADS_PER_STEP, n_tiles),
        in_specs=[pl.BlockSpec((1, TILE, width), lambda b, p, t: (b, t, p)),
                  pl.BlockSpec((1, ROWS, width), lambda b, p, t: (b, 0, p)),
                  pl.BlockSpec((1, ROWS, width), lambda b, p, t: (b, 0, p))],
        out_specs=pl.BlockSpec((1, TILE, width // 2), lambda b, p, t: (b, t, p)),
        out_shape=jax.ShapeDtypeStruct((B, n_tiles * TILE, MLA_HEADS * MLA_V), BF16),
        compiler_params=_params(("arbitrary", "arbitrary", "arbitrary")),
        name="mla_attn",
    )(q, k, v)


def _na_kernel(q_ref, k_ref, v_ref, bias_ref, o_ref):
    g = pl.program_id(1)
    lane = lax.broadcasted_iota(jnp.int32, (1, 2 * NA_HEAD_DIM), 1)
    nk_win = NA_KROWS * GRID_W

    def mask(h):
        return (lane >= NA_HEAD_DIM) if (h % 2) else (lane < NA_HEAD_DIM)

    def run(win_start):
        def scores(h):
            sl = slice((h // 2) * 128, (h // 2 + 1) * 128)
            qm = jnp.where(mask(h), q_ref[0, :, sl], 0).astype(BF16)
            s_c = lax.dot_general(qm, k_ref[0, SEQ:ROWS, sl], (((1,), (1,)), ((), ())),
                                  preferred_element_type=F32)
            if win_start is None:
                return None, s_c
            kw = k_ref[0, pl.ds(win_start, nk_win), sl]
            s_w = lax.dot_general(qm, kw, (((1,), (1,)), ((), ())),
                                  preferred_element_type=F32) + bias_ref[0, h]
            return s_w, s_c

        def head_out(h, s_w, s_c):
            sl = slice((h // 2) * 128, (h // 2 + 1) * 128)
            mk = mask(h)
            m = jnp.max(s_c, axis=-1, keepdims=True)
            if s_w is not None:
                m = jnp.maximum(m, jnp.max(s_w, axis=-1, keepdims=True))
            p_c = jnp.exp2(s_c - m)
            l = jnp.sum(p_c, axis=-1, keepdims=True)
            o = jnp.dot(p_c.astype(BF16), jnp.where(mk, v_ref[0, SEQ:ROWS, sl], 0).astype(BF16),
                        preferred_element_type=F32)
            if s_w is not None:
                p_w = jnp.exp2(s_w - m)
                l = l + jnp.sum(p_w, axis=-1, keepdims=True)
                vw = v_ref[0, pl.ds(win_start, nk_win), sl]
                o = o + jnp.dot(p_w.astype(BF16), jnp.where(mk, vw, 0).astype(BF16),
                                preferred_element_type=F32)
            return o * (1.0 / l)

        s = [scores(0)] + [None] * (NA_HEADS - 1)
        outs = [None] * NA_HEADS
        for h in range(NA_HEADS):
            if h + 1 < NA_HEADS:
                s[h + 1] = scores(h + 1)
            outs[h] = head_out(h, *s[h])
            s[h] = None
            if h % 2 == 1:
                sl = slice((h // 2) * 128, (h // 2 + 1) * 128)
                o_ref[0, :, sl] = (outs[h - 1] + outs[h]).astype(BF16)

    @pl.when(g < N_LAT_TILES)
    def _():
        ks = jnp.clip(g * NA_GROUP - NA_WIN_R // 2, 0, GRID_ROWS - NA_KROWS)
        run(pl.multiple_of(ks * GRID_W, GRID_W))

    @pl.when(g == N_LAT_TILES)
    def _():
        run(None)


def _na_attn(P, bias_tab, n_tiles):
    B = P.shape[0]
    width = NA_HEADS * NA_HEAD_DIM
    nq, nk = NA_GROUP * GRID_W, NA_KROWS * GRID_W

    def bias_map(b, g):
        return (jnp.where(g == 0, 0, jnp.where(g >= N_LAT_TILES - 1, 2, 1)), 0, 0, 0)

    return pl.pallas_call(
        _na_kernel,
        grid=(B, n_tiles),
        in_specs=[pl.BlockSpec((1, TILE, width), lambda b, g: (b, g, OFF_NA_Q // width)),
                  pl.BlockSpec((1, ROWS, width), lambda b, g: (b, 0, OFF_NA_K // width)),
                  pl.BlockSpec((1, ROWS, width), lambda b, g: (b, 0, OFF_NA_V // width)),
                  pl.BlockSpec((1, NA_HEADS, nq, nk), bias_map)],
        out_specs=pl.BlockSpec((1, TILE, width), lambda b, g: (b, g, 0)),
        out_shape=jax.ShapeDtypeStruct((B, n_tiles * TILE, width), BF16),
        compiler_params=_params(("arbitrary", "arbitrary")),
        name="na_attn",
    )(P, P, P, bias_tab)


def _na_bias_table(rpb):
    onehot, col_mask, drow_blocks = _na_bias_index()
    rc = jnp.dot(rpb.astype(F32).reshape(-1, 2 * NA_WIN_C - 1), jnp.asarray(onehot),
                 precision=lax.Precision.HIGHEST)
    rc = rc.reshape(NA_HEADS, 2 * NA_WIN_R - 1, GRID_W, GRID_W) * math.log2(math.e)
    rc = jnp.where(col_mask[None, None], rc, NA_NEG)
    neg = jnp.full((NA_HEADS, GRID_W, GRID_W), NA_NEG, F32)
    pats = []
    for p in range(3):
        rows = []
        for i in range(NA_GROUP):
            rows.append(jnp.concatenate(
                [neg if d < 0 else rc[:, d] for d in drow_blocks[p][i]], axis=-1))
        pats.append(jnp.concatenate(rows, axis=1))
    return jnp.stack(pats, axis=0)


def _hy_filter_kernel(z_ref, tn_ref, dl_ref, w1_ref, b1_ref, fr_ref, w2_ref, b2_ref, w3_ref, o_ref, a_ref):
    hp = lax.Precision.HIGHEST

    @pl.when(pl.program_id(0) == 0)
    def _():
        fr = fr_ref[...]
        a = jnp.sin(fr[0:1, :] * (jnp.dot(z_ref[...], w1_ref[...], preferred_element_type=F32, precision=hp)
                                  + b1_ref[...]))
        a_ref[...] = jnp.sin(fr[1:2, :] * (jnp.dot(a, w2_ref[...], preferred_element_type=F32, precision=hp)
                                           + b2_ref[...]))

    filt = jnp.dot(a_ref[...], w3_ref[...], preferred_element_type=F32, precision=hp)
    window = jnp.exp(-tn_ref[...] * dl_ref[...])
    f0 = filt[:, 0:HY_CT] * window
    f1 = filt[:, HY_CT:] * window
    den = (jnp.sum(jnp.abs(f0), axis=0, keepdims=True) + jnp.sum(jnp.abs(f1), axis=0, keepdims=True)) + EPS
    o_ref[:, 0:HY_CT] = f0 / den
    o_ref[:, HY_CT:] = f1 / den


def _hy_filters(L, w1p, b1, freq, w2, b2, w3):
    zp, t_norm, deltas = (jnp.asarray(t) for t in _hyena_pos_features(L))
    n_ct = HY_WIDTH // HY_CT
    w3t = w3.reshape(HY_FFN, 2, 2, n_ct, HY_CT).transpose(0, 1, 3, 2, 4).reshape(HY_FFN, 4 * HY_WIDTH)
    const = lambda n: (0, 0)
    return pl.pallas_call(
        _hy_filter_kernel,
        grid=(2 * n_ct,),
        in_specs=[pl.BlockSpec((L, 128), const), pl.BlockSpec((L, 1), const),
                  pl.BlockSpec((1, HY_CT), lambda n: (0, n % n_ct)), pl.BlockSpec((128, HY_FFN), const),
                  pl.BlockSpec((1, HY_FFN), const), pl.BlockSpec((2, HY_FFN), const),
                  pl.BlockSpec((HY_FFN, HY_FFN), const), pl.BlockSpec((1, HY_FFN), const),
                  pl.BlockSpec((HY_FFN, 2 * HY_CT), lambda n: (0, n))],
        out_specs=pl.BlockSpec((L, 2 * HY_CT), lambda n: (0, n)),
        out_shape=jax.ShapeDtypeStruct((L, 4 * HY_WIDTH), F32),
        scratch_shapes=[pltpu.VMEM((L, HY_FFN), F32)],
        compiler_params=_params(("arbitrary",)),
        name="hyena_filters",
    )(zp, t_norm, deltas, w1p, b1.reshape(1, -1), freq, w2, b2.reshape(1, -1), w3t)


def _hy_spectrum_kernel(f_ref, c2_ref, s2_ref, cp_ref, sp_ref, o_ref, *, L):
    half = f_ref.shape[1] // 2
    row = lax.broadcasted_iota(jnp.int32, (L, half), 0)
    ff = f_ref[:, 0:half]
    fb = jnp.where(row == 0, 0.0, f_ref[:, half:])

    fs = (ff + fb).astype(BF16)
    fd = (ff - fb).astype(BF16)
    a_s = jnp.dot(c2_ref[...], fs, preferred_element_type=F32)
    b_s = jnp.dot(s2_ref[...], fs, preferred_element_type=F32)
    a_d = jnp.dot(c2_ref[...], fd, preferred_element_type=F32)
    b_d = jnp.dot(s2_ref[...], fd, preferred_element_type=F32)
    cp, sp = cp_ref[...], sp_ref[...]
    scale = 1.0 / L
    o_ref[0] = (cp * a_s + sp * b_s) * scale
    o_ref[1] = (sp * a_d - cp * b_d) * scale


def _hy_spectrum(filt, L):
    c2, s2, cphi, sphi = _dft_tables(L)
    c2, s2 = jnp.asarray(c2).astype(BF16), jnp.asarray(s2).astype(BF16)
    n_ct = HY_WIDTH // HY_CT
    ft = min(L, 512)
    out = pl.pallas_call(
        functools.partial(_hy_spectrum_kernel, L=L),
        grid=(2 * n_ct, L // ft),
        in_specs=[pl.BlockSpec((L, 2 * HY_CT), lambda n, f: (0, n)),
                  pl.BlockSpec((ft, L), lambda n, f: (f, 0)),
                  pl.BlockSpec((ft, L), lambda n, f: (f, 0)),
                  pl.BlockSpec((ft, 1), lambda n, f: (f, 0)),
                  pl.BlockSpec((ft, 1), lambda n, f: (f, 0))],
        out_specs=pl.BlockSpec((2, ft, HY_CT), lambda n, f: (0, f, n)),
        out_shape=jax.ShapeDtypeStruct((2, L, 2 * n_ct * HY_CT), F32),
        compiler_params=_params(("arbitrary", "arbitrary")),
        name="hyena_spectrum",
    )(filt, c2, s2, jnp.asarray(cphi), jnp.asarray(sphi))
    return out, c2, s2


def _hy_conv_kernel(x1_ref, x2_ref, z_ref, w_ref, b_ref, skip_ref, h_ref, c2_ref, s2_ref, o_ref,
                    yre_ref, yim_ref, *, L):
    row = lax.broadcasted_iota(jnp.int32, (L, HY_CT), 0)

    def short_conv(p_ref, i):
        p = p_ref[0].astype(F32)
        w = w_ref[i]
        prev = jnp.where(row == 0, 0.0, pltpu.roll(p, 1, 0))
        nxt = jnp.where(row == L - 1, 0.0, pltpu.roll(p, L - 1, 0))
        return prev * w[0:1, :] + p * w[1:2, :] + nxt * w[2:3, :] + b_ref[i]

    ft = min(L, 512)
    z = short_conv(z_ref, 2)
    for n, gate_ref in enumerate((x1_ref, x2_ref)):
        zb = z.astype(BF16)
        for f in range(L // ft):
            fs = slice(f * ft, (f + 1) * ft)
            ur = jnp.dot(c2_ref[fs, :], zb, preferred_element_type=F32)
            us = jnp.dot(s2_ref[fs, :], zb, preferred_element_type=F32)
            hre, him = h_ref[0, n, fs, :], h_ref[1, n, fs, :]
            yre_ref[fs, :] = (ur * hre + us * him).astype(BF16)
            yim_ref[fs, :] = (ur * him - us * hre).astype(BF16)
        y = (jnp.dot(c2_ref[...], yre_ref[...], preferred_element_type=F32)
             - jnp.dot(s2_ref[...], yim_ref[...], preferred_element_type=F32))
        z = short_conv(gate_ref, n) * (y + z * skip_ref[n])
    o_ref[0] = z.astype(BF16)


def _hy_conv(P, row_blk, L, conv_w, conv_b, skip, spec, c2, s2):
    B = P.shape[0]
    n_ct = HY_WIDTH // HY_CT
    w4 = conv_w.reshape(3, 3, n_ct, HY_CT).transpose(2, 1, 0, 3)
    b4 = conv_b.reshape(3, n_ct, 1, HY_CT).transpose(1, 0, 2, 3)
    sk4 = skip.reshape(2, n_ct, 1, HY_CT).transpose(1, 0, 2, 3)
    h5 = spec.reshape(2, L, 2, n_ct, HY_CT).transpose(3, 0, 2, 1, 4)

    def pcol(off):
        return pl.BlockSpec((1, L, HY_CT), lambda c, b: (b, row_blk, off // HY_CT + c))

    const = lambda c, b: (0, 0)
    return pl.pallas_call(
        functools.partial(_hy_conv_kernel, L=L),
        grid=(n_ct, B),
        in_specs=[pcol(OFF_HY_X1), pcol(OFF_HY_X2), pcol(OFF_HY_ZIN),
                  pl.BlockSpec((None, 3, 3, HY_CT), lambda c, b: (c, 0, 0, 0)),
                  pl.BlockSpec((None, 3, 1, HY_CT), lambda c, b: (c, 0, 0, 0)),
                  pl.BlockSpec((None, 2, 1, HY_CT), lambda c, b: (c, 0, 0, 0)),
                  pl.BlockSpec((None, 2, 2, L, HY_CT), lambda c, b: (c, 0, 0, 0, 0),
                               pipeline_mode=pl.Buffered(1)),
                  _single((L, L), const), _single((L, L), const)],
        out_specs=pl.BlockSpec((1, L, HY_CT), lambda c, b: (b, 0, c)),
        out_shape=jax.ShapeDtypeStruct((B, L, HY_WIDTH), BF16),
        scratch_shapes=[pltpu.VMEM((L, HY_CT), BF16), pltpu.VMEM((L, HY_CT), BF16)],
        compiler_params=_params(("arbitrary", "arbitrary")),
        name="hyena_conv",
    )(P, P, P, w4, b4, sk4, h5, c2, s2)


def _gla_kernel(qk_ref, v_ref, misc_ref, wg_ref, bg_ref, ng_ref, o_ref,
                qe_s, ke_s, qi_s, ko_s, dec_s, o_scr, s_scr):
    C = GLA_CHUNK
    HK = GLA_HEADS * GLA_DK
    NC = GLA_NCHUNK
    logit = jnp.dot(misc_ref[0], wg_ref[...], preferred_element_type=F32) + bg_ref[...]
    g = (jnp.minimum(logit, 0.0) - jnp.log(1.0 + jnp.exp(-jnp.abs(logit)))) * (1.0 / GLA_TAU)
    rmod = lax.broadcasted_iota(jnp.int32, (ROWS, 1), 0) % C
    p = g
    shift = 1
    while shift < C:
        p = p + jnp.where(rmod >= shift, pltpu.roll(p, shift, 0), 0.0)
        shift *= 2
    p3 = p.reshape(NC, C, 2 * HK)
    g3 = g.reshape(NC, C, 2 * HK)
    tot = p3[:, C - 1:C, :]
    q3 = (qk_ref[0, :, 0:HK].astype(F32) * (GLA_DK ** -0.5)).reshape(NC, C, HK)
    k3 = qk_ref[0, :, HK:2 * HK].astype(F32).reshape(NC, C, HK)
    for d in range(2):
        cs = slice(d * HK, (d + 1) * HK)
        if d == 0:
            bc = p3[:, :, cs]
            bm = bc[:, C // 2 - 1:C // 2, :]
        else:
            bc = tot[:, :, cs] - p3[:, :, cs] + g3[:, :, cs]
            bm = bc[:, C // 2:C // 2 + 1, :]
        bl = tot[:, :, cs]
        qe = q3 * jnp.exp(bc - bm)
        ke = k3 * jnp.exp(bm - bc)
        qe_s[d] = qe.reshape(ROWS, HK).astype(BF16)
        ke_s[d] = ke.reshape(ROWS, HK).astype(BF16)
        qi_s[d] = (qe * jnp.exp(bm)).reshape(ROWS, HK).astype(BF16)
        ko_s[d] = (ke * jnp.exp(bl - bm)).reshape(ROWS, HK).astype(BF16)
        dec_s[d] = jnp.broadcast_to(jnp.exp(bl), (NC, 8, HK)).reshape(NC * 8, HK)
    s_scr[...] = jnp.zeros_like(s_scr)

    ri = lax.broadcasted_iota(jnp.int32, (GLA_HEADS * C, C), 0) % C
    ci = lax.broadcasted_iota(jnp.int32, (GLA_HEADS * C, C), 1)
    keep = (ri >= ci, ri <= ci)
    lane = lax.broadcasted_iota(jnp.int32, (1, HK), 1)
    lane2 = lax.broadcasted_iota(jnp.int32, (1, 2 * GLA_DK), 1)

    def intra_scores(d, c):
        rows = pl.ds(pl.multiple_of(c * C, C), C)
        qe = qe_s[d, rows, :]
        stack = jnp.concatenate(
            [jnp.where((lane >= h * GLA_DK) & (lane < (h + 1) * GLA_DK), qe, 0).astype(BF16)
             for h in range(GLA_HEADS)], axis=0)
        a_all = lax.dot_general(stack, ke_s[d, rows, :], (((1,), (1,)), ((), ())),
                                preferred_element_type=F32)
        return jnp.where(keep[d], a_all, 0.0).astype(BF16)

    def state_terms(d, c):
        rows = pl.ds(pl.multiple_of(c * C, C), C)
        qi = qi_s[d, rows, :]
        ko = ko_s[d, rows, :]
        dec = dec_s[d, pl.ds(pl.multiple_of(c * 8, 8), 8), :][0:1, :]
        inter = []
        for h in range(GLA_HEADS):
            sl = slice((h // 2) * 128, (h // 2 + 1) * 128)
            mk = (lane2 >= GLA_DK) if (h % 2) else (lane2 < GLA_DK)
            v_h = v_ref[0, rows, h * GLA_DV:(h + 1) * GLA_DV]
            st = s_scr[d, h]
            qi_h = jnp.where(mk, qi[:, sl], 0).astype(BF16)
            inter.append(lax.dot_general(qi_h, st.astype(BF16), (((1,), (1,)), ((), ())),
                                         preferred_element_type=F32))
            ko_h = jnp.where(mk, ko[:, sl], 0).astype(BF16)
            kv_t = lax.dot_general(v_h, ko_h, (((0,), (0,)), ((), ())), preferred_element_type=F32)
            s_scr[d, h] = st * dec[:, sl] + kv_t
        return inter

    def write_out(d, c, a_all, inter):
        rows = pl.ds(pl.multiple_of(c * C, C), C)
        for h in range(GLA_HEADS):
            v_h = v_ref[0, rows, h * GLA_DV:(h + 1) * GLA_DV]
            o_scr[d, rows, h * GLA_DV:(h + 1) * GLA_DV] = (
                jnp.dot(a_all[h * C:(h + 1) * C], v_h, preferred_element_type=F32) + inter[h])

    def step(i, carry):
        cf = jnp.where(i < GLA_CTX_CHUNKS, GLA_NCHUNK - GLA_CTX_CHUNKS + i, i - GLA_CTX_CHUNKS)
        cb = GLA_NCHUNK - 1 - i
        a_f = intra_scores(0, cf)
        a_b = intra_scores(1, cb)
        i_f = state_terms(0, cf)
        i_b = state_terms(1, cb)
        write_out(0, cf, a_f, i_f)
        write_out(1, cb, a_b, i_b)
        return carry

    lax.fori_loop(0, GLA_NCHUNK, step, 0, unroll=2)

    ng = ng_ref[...]
    for t in range(N_TILES):
        rs = slice(t * TILE, (t + 1) * TILE)
        for h in range(GLA_HEADS):
            cs = slice(h * GLA_DV, (h + 1) * GLA_DV)
            o = o_scr[0, rs, cs] + o_scr[1, rs, cs]
            o = o * lax.rsqrt(jnp.mean(o * o, axis=-1, keepdims=True) + EPS) * ng
            o_ref[0, rs, cs] = o.astype(BF16)


def _gla(P, wg, bg, norm_g):
    B = P.shape[0]
    width = GLA_HEADS * GLA_DV
    const = lambda b: (0, 0)
    return pl.pallas_call(
        _gla_kernel,
        grid=(B,),
        in_specs=[pl.BlockSpec((1, ROWS, 512), lambda b: (b, 0, OFF_GLA_QK // 512)),
                  pl.BlockSpec((1, ROWS, width), lambda b: (b, 0, OFF_GLA_V // width)),
                  pl.BlockSpec((1, ROWS, 128), lambda b: (b, 0, (OFF_MLA + 384) // 128)),
                  pl.BlockSpec((128, 2 * GLA_HEADS * GLA_DK), const),
                  pl.BlockSpec((1, 2 * GLA_HEADS * GLA_DK), const),
                  pl.BlockSpec((1, GLA_DV), const)],
        out_specs=pl.BlockSpec((1, ROWS, width), lambda b: (b, 0, 0)),
        out_shape=jax.ShapeDtypeStruct((B, ROWS, width), BF16),
        scratch_shapes=[pltpu.VMEM((2, ROWS, GLA_HEADS * GLA_DK), BF16),
                        pltpu.VMEM((2, ROWS, GLA_HEADS * GLA_DK), BF16),
                        pltpu.VMEM((2, ROWS, GLA_HEADS * GLA_DK), BF16),
                        pltpu.VMEM((2, ROWS, GLA_HEADS * GLA_DK), BF16),
                        pltpu.VMEM((2, GLA_NCHUNK * 8, GLA_HEADS * GLA_DK), F32),
                        pltpu.VMEM((2, ROWS, width), F32),
                        pltpu.VMEM((2, GLA_HEADS, GLA_DV, 2 * GLA_DK), F32)],
        compiler_params=_params(("arbitrary",)),
        name="gla",
    )(P, P, P, wg, bg, norm_g.reshape(1, -1))


def _merge_kernel(ya_ref, yb_ref, yh_ref, yg_ref, za_ref, zb_ref, zh_ref, zg_ref, gt_ref,
                  x_ref, mod_ref, pg_ref, wb_ref, wo_ref, o_ref):
    merged = None
    one = jnp.asarray(1.0, BF16)
    for i, (y_ref, z_ref) in enumerate(((ya_ref, za_ref), (yb_ref, zb_ref), (yh_ref, zh_ref), (yg_ref, zg_ref))):
        z = z_ref[0]
        t = y_ref[0] * (z * (jnp.tanh(z) + one))
        br = jnp.dot(t, wb_ref[i], preferred_element_type=F32)
        gl = gt_ref[0, :, i * D_MODEL:(i + 1) * D_MODEL].astype(F32)
        term = br * (jnp.tanh(gl) + 1.0)
        merged = term if merged is None else merged + term
    out = jnp.dot(merged.astype(BF16), wo_ref[...], preferred_element_type=F32)
    xn = out * lax.rsqrt(jnp.mean(out * out, axis=-1, keepdims=True) + EPS) * pg_ref[...]
    gate = mod_ref[0][:, 2 * D_MODEL:3 * D_MODEL]
    o_ref[0] = x_ref[0] + gate * xn


def _merge(x, P, ya, yb, yh, yg, mod3, mod_row, post_g, wb, wo, row_off):
    B, R, _ = x.shape
    n_t = R // TILE
    const2 = lambda b, t: (0, 0)

    def uni(width, off):
        return pl.BlockSpec((1, TILE, width), lambda b, t: (b, t + row_off, off // width))

    return pl.pallas_call(
        _merge_kernel,
        grid=(B, n_t),
        in_specs=[uni(512, 0), uni(512, 0),
                  pl.BlockSpec((1, TILE, 512), lambda b, t: (b, t, 0)),
                  uni(512, 0),
                  uni(512, OFF_MLA_Z), uni(512, OFF_NA_Z), uni(512, OFF_HY_Z), uni(512, OFF_GLA_Z),
                  uni(4 * D_MODEL, OFF_MERGE),
                  pl.BlockSpec((1, TILE, D_MODEL), lambda b, t: (b, t, 0)),
                  pl.BlockSpec((1, 1, 3 * D_MODEL), lambda b, t: (mod_row(b), 0, 0)),
                  pl.BlockSpec((1, D_MODEL), const2),
                  pl.BlockSpec((4, 512, D_MODEL), lambda b, t: (0, 0, 0)),
                  pl.BlockSpec((D_MODEL, D_MODEL), const2)],
        out_specs=pl.BlockSpec((1, TILE, D_MODEL), lambda b, t: (b, t, 0)),
        out_shape=jax.ShapeDtypeStruct((B, R, D_MODEL), F32),
        compiler_params=_params(("arbitrary", "arbitrary")),
        name="merge",
    )(ya, yb, yh, yg, P, P, P, P, P, x, mod3, post_g.reshape(1, -1), wb, wo)


def _build_w_big(w):
    def cols(name):
        a, b = _ref_cols(name)
        return w[:, a:b]

    kr = cols('mla_kr')
    krx = kr[:, _ROPE_PERM] * _ROPE_SIGN
    pad32 = jnp.zeros((D_MODEL, 32), w.dtype)
    na = cols('na_qkv')
    nw = NA_HEADS * NA_HEAD_DIM
    na = jnp.concatenate([na[:, :nw] * (NA_HEAD_DIM ** -0.5 * math.log2(math.e)), na[:, nw:]], axis=1)
    parts = [0.5 * cols('merge'), cols('mla_cq'), cols('mla_ckv'), kr, cols('gla_glr'), krx, pad32,
             0.5 * cols('mla_z'), na, 0.5 * cols('na_z'), cols('hy_proj'), 0.5 * cols('hy_z'),
             cols('gla_qk'), cols('gla_v'), 0.5 * cols('gla_z')]
    big = jnp.concatenate(parts, axis=1).astype(BF16)
    assert big.shape == (D_MODEL, P_COLS)
    return big


def _build_mla_weights(q_norm, w_uq, kv_norm, w_ukv):
    dq = MLA_NOPE + MLA_ROPE
    wq3 = w_uq.reshape(MLA_Q_RANK, MLA_HEADS, dq)
    rot = wq3[:, :, MLA_NOPE:]
    rotx = rot[:, :, _ROPE_PERM] * _ROPE_SIGN
    wq = jnp.concatenate([wq3, rotx], axis=-1).reshape(MLA_Q_RANK, MLA_HEADS * MLA_SLOT)
    wkv3 = w_ukv.reshape(MLA_KV_RANK, MLA_HEADS, MLA_NOPE + MLA_V)
    zk = jnp.zeros((MLA_KV_RANK, MLA_HEADS, MLA_SLOT - MLA_NOPE), F32)
    wk = jnp.concatenate([wkv3[:, :, :MLA_NOPE], zk], axis=-1).reshape(MLA_KV_RANK, MLA_HEADS * MLA_SLOT)
    wv3 = wkv3[:, :, MLA_NOPE:]
    zv = jnp.zeros_like(wv3)
    even = jnp.concatenate([wv3, zv], axis=-1)
    odd = jnp.concatenate([zv, wv3], axis=-1)
    sel = (np.arange(MLA_HEADS) % 2 == 1)[None, :, None]
    wv = jnp.where(sel, odd, even).reshape(MLA_KV_RANK, MLA_HEADS * MLA_SLOT)
    e1 = np.zeros((128, MLA_SLOT), np.float32)
    for i in range(MLA_ROPE):
        e1[i, MLA_NOPE + i] = 1.0
        e1[64 + i, MLA_NOPE + MLA_ROPE + i] = 1.0
    return dict(q_norm=q_norm.reshape(1, -1), kv_norm=kv_norm.reshape(1, -1),
                wq=wq.astype(BF16), wk=wk.astype(BF16), wv=wv.astype(BF16),
                e1=jnp.asarray(e1, BF16))


def _build_gla_gate_weights(wg2, bg):
    hk = GLA_HEADS * GLA_DK
    wg = jnp.zeros((128, 2 * hk), F32)
    for i in range(2):
        wg = wg.at[32 + i * GLA_RANK:32 + (i + 1) * GLA_RANK, i * hk:(i + 1) * hk].set(wg2[i])
    return wg.astype(BF16), bg.reshape(1, 2 * hk)


def kernel(x, c, ctx, c_ctx, ada_w, ada_b, pre_g, post_g, w_in, mla_q_norm, mla_w_uq, mla_kv_norm, mla_w_ukv, na_rpb, hy_conv_w, hy_conv_b, hy_pe_w1, hy_pe_b1, hy_pe_freq, hy_pe_w2, hy_pe_b2, hy_pe_w3, hy_skip, gla_wg2, gla_bg, gla_norm, w_branch, w_out):
    B = x.shape[0]
    depth = ada_w.shape[0]
    mod_rows = -(-(B + 1) // 8) * 8
    cc = jnp.zeros((mod_rows, D_MODEL), F32).at[:B].set(c).at[B].set(c_ctx)
    cx = ctx
    for l in range(depth):
        need_ctx = l < depth - 1
        n_tiles = N_TILES if need_ctx else N_LAT_TILES
        mod3 = _adaln(cc, ada_w[l], ada_b[l]).reshape(mod_rows, 1, 3 * D_MODEL)
        P = _inproj(x, cx, mod3, pre_g[l], _build_w_big(w_in[l]))

        mla_w = _build_mla_weights(mla_q_norm[l], mla_w_uq[l], mla_kv_norm[l], mla_w_ukv[l])
        q, k, v = _mla_prep(P, mla_w, N_TILES)
        ya = _mla_attn(q, k, v, n_tiles)

        yb = _na_attn(P, _na_bias_table(na_rpb[l]), n_tiles)

        w1p = jnp.zeros((128, HY_FFN), F32).at[:HY_EMB].set(hy_pe_w1[l])

        def hyena(L, row_blk):
            filt = _hy_filters(L, w1p, hy_pe_b1[l], hy_pe_freq[l], hy_pe_w2[l], hy_pe_b2[l], hy_pe_w3[l])
            spec, c2, s2 = _hy_spectrum(filt, L)
            return _hy_conv(P, row_blk, L, hy_conv_w[l], hy_conv_b[l], hy_skip[l], spec, c2, s2)

        yh = hyena(SEQ, 0)

        wg, bgp = _build_gla_gate_weights(gla_wg2[l], gla_bg[l])
        yg = _gla(P, wg, bgp, gla_norm[l])

        wb = (0.5 * w_branch[l]).astype(BF16)
        wo = w_out[l].astype(BF16)
        if need_ctx:
            yhc = hyena(CTX_LEN, SEQ // CTX_LEN)
            cx = _merge(cx, P, ya, yb, yhc, yg, mod3, lambda b: B, post_g[l], wb, wo, N_LAT_TILES)
        x = _merge(x, P, ya, yb, yh, yg, mod3, lambda b: b, post_g[l], wb, wo, 0)
    return x
```

```python
import functools
import math

import numpy as np
import jax
import jax.numpy as jnp
from jax import lax
from jax.experimental import pallas as pl
from jax.experimental.pallas import tpu as pltpu

F32 = jnp.float32
BF16 = jnp.bfloat16

D_MODEL = 1024
SEQ = 2048
CTX_LEN = 256
ROWS = SEQ + CTX_LEN
GRID_W = 64
GRID_ROWS = SEQ // GRID_W
EPS = 1e-6
ROPE_BASE = 10000.0
TILE = 256
N_LAT_TILES = SEQ // TILE
N_TILES = ROWS // TILE

MLA_HEADS = 8
MLA_NOPE = 64
MLA_ROPE = 32
MLA_V = 64
MLA_Q_RANK = 256
MLA_KV_RANK = 128
MLA_SLOT = 128
MLA_HEADS_PER_STEP = 4
MLA_LOOKAHEAD = 2

NA_HEADS = 8
NA_HEAD_DIM = 64
NA_WIN_R = 8
NA_WIN_C = 16
NA_GROUP = 4
NA_KROWS = NA_GROUP + NA_WIN_R - 1
NA_NEG = -1e30

HY_WIDTH = 512
HY_EMB = 33
HY_FFN = 64
HY_TARGET = 1e-2
HY_FAST_DECAY_PCT = 0.3
HY_SLOW_DECAY_PCT = 1.5
HY_CT = 256

GLA_HEADS = 4
GLA_DK = 64
GLA_DV = 128
GLA_RANK = 16
GLA_TAU = 16.0
GLA_CHUNK = 64
GLA_NCHUNK = ROWS // GLA_CHUNK
GLA_CTX_CHUNKS = CTX_LEN // GLA_CHUNK

OFF_MERGE = 0
OFF_MLA = 4096
OFF_MLA_Z = 4608
OFF_NA_Q = 5120
OFF_NA_K = 5632
OFF_NA_V = 6144
OFF_NA_Z = 6656
OFF_HY_X1 = 7168
OFF_HY_X2 = 7680
OFF_HY_ZIN = 8192
OFF_HY_Z = 8704
OFF_GLA_QK = 9216
OFF_GLA_V = 9728
OFF_GLA_Z = 10240
P_COLS = 10752
INPROJ_TN = 512

VMEM_LIMIT = 56 * 1024 * 1024

_REF_LAYOUT = (
    ('mla_cq', 256), ('mla_ckv', 128), ('mla_kr', 32), ('mla_z', 512), ('na_qkv', 1536),
    ('na_z', 512), ('hy_proj', 1536), ('hy_z', 512), ('gla_qk', 512), ('gla_v', 512),
    ('gla_glr', 32), ('gla_z', 512), ('merge', 4096),
)


def _ref_cols(name):
    start = 0
    for n, w in _REF_LAYOUT:
        if n == name:
            return start, start + w
        start += w
    raise KeyError(name)


def _params(sem, vmem=VMEM_LIMIT):
    return pltpu.CompilerParams(dimension_semantics=sem, vmem_limit_bytes=vmem)


def _single(shape, index_map):
    return pl.BlockSpec(shape, index_map, pipeline_mode=pl.Buffered(1))


_ROPE_PERM = np.concatenate([np.arange(8, 16), np.arange(0, 8), np.arange(24, 32), np.arange(16, 24)])
_ROPE_SIGN = np.concatenate([-np.ones(8), np.ones(8), -np.ones(8), np.ones(8)]).astype(np.float32)


@functools.lru_cache(maxsize=None)
def _rope_tables():
    t = np.arange(SEQ)
    row = (t // GRID_W).astype(np.float32)
    col = (t % GRID_W).astype(np.float32)
    half = MLA_ROPE // 2
    inv = np.power(np.float32(ROPE_BASE), -np.arange(0, half, 2, dtype=np.float32) / np.float32(half))
    ar, ac = row[:, None] * inv, col[:, None] * inv
    ang = np.concatenate([ar, ar, ac, ac], axis=-1).astype(np.float32)
    cos = np.ones((ROWS, MLA_ROPE), np.float32)
    sin = np.zeros((ROWS, MLA_ROPE), np.float32)
    cos[:SEQ] = np.cos(ang)
    sin[:SEQ] = np.sin(ang)
    scale = np.float32((MLA_NOPE + MLA_ROPE) ** -0.5 * math.log2(math.e))
    cos_q = np.zeros((ROWS, MLA_SLOT), np.float32)
    sin_q = np.zeros((ROWS, MLA_SLOT), np.float32)
    cos_q[:, :MLA_NOPE] = scale
    cos_q[:, MLA_NOPE:MLA_NOPE + MLA_ROPE] = cos * scale
    sin_q[:, MLA_NOPE:MLA_NOPE + MLA_ROPE] = sin * scale
    cos_k = np.zeros((ROWS, MLA_SLOT), np.float32)
    sin_k = np.zeros((ROWS, MLA_SLOT), np.float32)
    cos_k[:, MLA_NOPE:MLA_NOPE + MLA_ROPE] = cos
    sin_k[:, MLA_NOPE:MLA_NOPE + MLA_ROPE] = sin
    return cos_q, sin_q, cos_k, sin_k


@functools.lru_cache(maxsize=None)
def _dft_tables(L):
    n = 2 * L
    k = np.arange(L, dtype=np.int64)
    freq = np.concatenate([k[:L // 2], L - 1 - k[:L // 2]])
    prod = np.mod(np.outer(2 * freq + 1, 2 * k + 1), 4 * n)
    ang = prod.astype(np.float64) * (math.pi / (2 * n))
    c2 = np.cos(ang).astype(np.float32)
    s2 = np.sin(ang).astype(np.float32)
    phi = (freq.astype(np.float64) + 0.5) * (math.pi / n)
    cphi = np.cos(phi).astype(np.float32)[:, None]
    sphi = np.sin(phi).astype(np.float32)[:, None]
    ce, co = c2[:, 0::2], c2[:, 1::2]
    return dict(c2=c2, s2=s2, cphi=cphi, sphi=sphi, ce=np.ascontiguousarray(ce), co=np.ascontiguousarray(co),
                cet=np.ascontiguousarray(ce.T), cot=np.ascontiguousarray(co.T))


@functools.lru_cache(maxsize=None)
def _hyena_pos_features(L):
    f32 = np.float32
    t = np.arange(L, dtype=f32)
    t_norm = t / f32(max(L - 1, 1))
    bands = (HY_EMB - 1) // 2
    fr = np.linspace(1e-4, bands - 1, bands, dtype=f32)
    ang = (f32(2.0 * math.pi / L) * t[:, None] * fr[None, :]).astype(f32)
    z = np.concatenate([t_norm[:, None], np.cos(ang), -np.sin(ang)], axis=-1).astype(f32)
    zp = np.zeros((L, 128), f32)
    zp[:, :HY_EMB] = z
    max_decay = math.log(HY_TARGET) / HY_FAST_DECAY_PCT
    min_decay = math.log(HY_TARGET) / HY_SLOW_DECAY_PCT
    deltas = np.abs(np.linspace(min_decay, max_decay, HY_WIDTH, dtype=f32)).astype(f32)
    return zp, t_norm[:, None].astype(f32), deltas[None, :]


@functools.lru_cache(maxsize=None)
def _na_bias_index():
    wr = NA_WIN_R
    j = np.arange(GRID_W)
    c0 = np.clip(j - NA_WIN_C // 2, 0, GRID_W - NA_WIN_C)
    col_mask = (j[None, :] >= c0[:, None]) & (j[None, :] < c0[:, None] + NA_WIN_C)
    dcol = np.clip(j[None, :] - j[:, None], -(NA_WIN_C - 1), NA_WIN_C - 1) + NA_WIN_C - 1
    onehot = np.zeros((2 * NA_WIN_C - 1, GRID_W * GRID_W), np.float32)
    onehot[dcol.reshape(-1), np.arange(GRID_W * GRID_W)] = 1.0
    blocks = []
    for ra in (0, NA_GROUP, GRID_ROWS - NA_GROUP):
        ks = int(np.clip(ra - wr // 2, 0, GRID_ROWS - NA_KROWS))
        per_row = []
        for i in range(NA_GROUP):
            r = ra + i
            r0 = int(np.clip(r - wr // 2, 0, GRID_ROWS - wr))
            per_row.append(tuple((ks + jj - r + wr - 1) if (r0 <= ks + jj < r0 + wr) else -1
                                 for jj in range(NA_KROWS)))
        blocks.append(tuple(per_row))
    return onehot, col_mask, tuple(blocks)


def _adaln_kernel(c_ref, w_ref, b_ref, o_ref):
    c = c_ref[...]
    a = c * (1.0 / (1.0 + jnp.exp(-c)))
    o_ref[...] = jnp.dot(a, w_ref[...], preferred_element_type=F32,
                         precision=lax.Precision.HIGHEST) + b_ref[...]


def _adaln(cc, w, b):
    rows = cc.shape[0]
    tn = 768
    return pl.pallas_call(
        _adaln_kernel,
        grid=(3 * D_MODEL // tn,),
        in_specs=[pl.BlockSpec((rows, D_MODEL), lambda n: (0, 0)),
                  pl.BlockSpec((D_MODEL, tn), lambda n: (0, n)),
                  pl.BlockSpec((1, tn), lambda n: (0, n))],
        out_specs=pl.BlockSpec((rows, tn), lambda n: (0, n)),
        out_shape=jax.ShapeDtypeStruct((rows, 3 * D_MODEL), F32),
        compiler_params=_params(("arbitrary",)),
        name="adaln",
    )(cc, w, b.reshape(1, -1))


def _inproj_kernel(x_ref, cx_ref, modl_ref, modc_ref, g_ref, w_ref, o_ref, h_ref):
    @pl.when(pl.program_id(1) == 0)
    def _():
        g = g_ref[...]

        def norm_mod(xv, mod):
            ms = jnp.mean(xv * xv, axis=-1, keepdims=True)
            xn = xv * lax.rsqrt(ms + EPS) * g
            return (xn * (1.0 + mod[:, D_MODEL:2 * D_MODEL]) + mod[:, 0:D_MODEL]).astype(BF16)

        modl = modl_ref[0]
        for i in range(N_LAT_TILES):
            h_ref[i * TILE:(i + 1) * TILE, :] = norm_mod(x_ref[0, i * TILE:(i + 1) * TILE, :], modl)
        h_ref[SEQ:ROWS, :] = norm_mod(cx_ref[0], modc_ref[0])

    o_ref[0] = jnp.dot(h_ref[...], w_ref[...], preferred_element_type=F32).astype(BF16)


def _inproj(x, cx, mod3, pre_g, w_big):
    B = x.shape[0]
    return pl.pallas_call(
        _inproj_kernel,
        grid=(B, P_COLS // INPROJ_TN),
        in_specs=[pl.BlockSpec((1, SEQ, D_MODEL), lambda b, n: (b, 0, 0)),
                  pl.BlockSpec((1, CTX_LEN, D_MODEL), lambda b, n: (b, 0, 0)),
                  pl.BlockSpec((1, 1, 3 * D_MODEL), lambda b, n: (b, 0, 0)),
                  pl.BlockSpec((1, 1, 3 * D_MODEL), lambda b, n: (B, 0, 0)),
                  pl.BlockSpec((1, D_MODEL), lambda b, n: (0, 0)),
                  pl.BlockSpec((D_MODEL, INPROJ_TN), lambda b, n: (0, n))],
        out_specs=pl.BlockSpec((1, ROWS, INPROJ_TN), lambda b, n: (b, 0, n)),
        out_shape=jax.ShapeDtypeStruct((B, ROWS, P_COLS), BF16),
        scratch_shapes=[pltpu.VMEM((ROWS, D_MODEL), BF16)],
        compiler_params=_params(("arbitrary", "arbitrary")),
        name="inproj",
    )(x, cx, mod3, mod3, pre_g.reshape(1, -1), w_big)


def _mla_prep_kernel(p_ref, qn_ref, kvn_ref, wq_ref, wk_ref, wv_ref, e1_ref,
                     cq_ref, sq_ref, ck_ref, sk_ref, q_ref, k_ref, v_ref):
    pm = p_ref[0]
    cq = pm[:, 0:MLA_Q_RANK].astype(F32)
    ckv = pm[:, MLA_Q_RANK:MLA_Q_RANK + MLA_KV_RANK].astype(F32)
    misc = pm[:, MLA_Q_RANK + MLA_KV_RANK:]
    cqn = (cq * lax.rsqrt(jnp.mean(cq * cq, axis=-1, keepdims=True) + EPS) * qn_ref[...]).astype(BF16)
    ckvn = (ckv * lax.rsqrt(jnp.mean(ckv * ckv, axis=-1, keepdims=True) + EPS) * kvn_ref[...]).astype(BF16)
    q = jnp.dot(cqn, wq_ref[...], preferred_element_type=F32)
    qx = pltpu.roll(q, MLA_HEADS * MLA_SLOT - MLA_ROPE, 1)
    kn = jnp.dot(ckvn, wk_ref[...], preferred_element_type=F32)
    v = jnp.dot(ckvn, wv_ref[...], preferred_element_type=F32)
    kr = jnp.dot(misc, e1_ref[...], preferred_element_type=F32)
    krot = kr * ck_ref[...] + pltpu.roll(kr, MLA_SLOT - MLA_ROPE, 1) * sk_ref[...]
    cosq, sinq = cq_ref[...], sq_ref[...]
    for h in range(MLA_HEADS):
        sl = slice(h * MLA_SLOT, (h + 1) * MLA_SLOT)
        q_ref[0, :, sl] = (q[:, sl] * cosq + qx[:, sl] * sinq).astype(BF16)
        k_ref[0, :, sl] = (kn[:, sl] + krot).astype(BF16)
    lane = lax.broadcasted_iota(jnp.int32, (1, MLA_HEADS * MLA_SLOT), 1)
    upper = (lane % MLA_SLOT) >= MLA_V
    odd = (lane // MLA_SLOT) % 2 == 1
    v_ref[0] = jnp.where(upper != odd, 1.0, v).astype(BF16)


def _mla_prep(P, wts, n_tiles):
    B = P.shape[0]
    width = MLA_HEADS * MLA_SLOT
    cos_q, sin_q, cos_k, sin_k = (jnp.asarray(t) for t in _rope_tables())
    const = lambda b, t: (0, 0)
    tab = pl.BlockSpec((TILE, MLA_SLOT), lambda b, t: (t, 0))
    out = pl.BlockSpec((1, TILE, width), lambda b, t: (b, t, 0))
    shp = jax.ShapeDtypeStruct((B, ROWS, width), BF16)
    return pl.pallas_call(
        _mla_prep_kernel,
        grid=(B, N_TILES),
        in_specs=[pl.BlockSpec((1, TILE, 512), lambda b, t: (b, t, OFF_MLA // 512)),
                  pl.BlockSpec((1, MLA_Q_RANK), const),
                  pl.BlockSpec((1, MLA_KV_RANK), const),
                  pl.BlockSpec((MLA_Q_RANK, width), const),
                  pl.BlockSpec((MLA_KV_RANK, width), const),
                  pl.BlockSpec((MLA_KV_RANK, width), const),
                  pl.BlockSpec((128, MLA_SLOT), const),
                  tab, tab, tab, tab],
        out_specs=[out, out, out],
        out_shape=[shp, shp, shp],
        compiler_params=_params(("arbitrary", "arbitrary")),
        name="mla_prep",
    )(P, wts['q_norm'], wts['kv_norm'], wts['wq'], wts['wk'], wts['wv'],
      wts['e1'], cos_q, sin_q, cos_k, sin_k)


def _mla_attn_kernel(q_ref, k_ref, v_ref, o_ref):
    lane = lax.broadcasted_iota(jnp.int32, (1, MLA_SLOT), 1)

    def attend(k0, nk):
        def scores(h):
            sl = slice(h * MLA_SLOT, (h + 1) * MLA_SLOT)
            return lax.dot_general(q_ref[0, :, sl], k_ref[0, k0:k0 + nk, sl], (((1,), (1,)), ((), ())),
                                   preferred_element_type=F32)

        def head_out(h, s):
            sl = slice(h * MLA_SLOT, (h + 1) * MLA_SLOT)
            m = jnp.max(s, axis=-1, keepdims=True)
            p = jnp.exp2(s - m).astype(BF16)
            o = jnp.dot(p, v_ref[0, k0:k0 + nk, sl], preferred_element_type=F32)
            r = pltpu.roll(o, MLA_V, 1)
            keep = (lane >= MLA_V) if (h % 2) else (lane < MLA_V)
            return jnp.where(keep, o * (1.0 / r), 0.0)

        nh = MLA_HEADS_PER_STEP
        s = [scores(h) if h < MLA_LOOKAHEAD else None for h in range(nh)]
        outs = [None] * nh
        for h in range(nh):
            if h + MLA_LOOKAHEAD < nh:
                s[h + MLA_LOOKAHEAD] = scores(h + MLA_LOOKAHEAD)
            outs[h] = head_out(h, s[h])
            s[h] = None
            if h % 2 == 1:
                pr = h // 2
                o_ref[0, :, pr * MLA_SLOT:(pr + 1) * MLA_SLOT] = (outs[h - 1] + outs[h]).astype(BF16)

    is_ctx = pl.program_id(2) == N_LAT_TILES

    @pl.when(jnp.logical_not(is_ctx))
    def _():
        attend(0, ROWS)

    @pl.when(is_ctx)
    def _():
        attend(SEQ, CTX_LEN)


def _mla_attn(q, k, v, n_tiles):
    B = q.shape[0]
    width = MLA_HEADS_PER_STEP * MLA_SLOT
    return pl.pallas_call(
        _mla_attn_kernel,
        grid=(B, MLA_HEADS // MLA_HEADS_PER_STEP, n_tiles),
        in_specs=[pl.BlockSpec((1, TILE, width), lambda b, p, t: (b, t, p)),
                  pl.BlockSpec((1, ROWS, width), lambda b, p, t: (b, 0, p)),
                  pl.BlockSpec((1, ROWS, width), lambda b, p, t: (b, 0, p))],
        out_specs=pl.BlockSpec((1, TILE, width // 2), lambda b, p, t: (b, t, p)),
        out_shape=jax.ShapeDtypeStruct((B, n_tiles * TILE, MLA_HEADS * MLA_V), BF16),
        compiler_params=_params(("arbitrary", "arbitrary", "arbitrary")),
        name="mla_attn",
    )(q, k, v)


def _na_kernel(q_ref, k_ref, v_ref, bias_ref, o_ref):
    g = pl.program_id(1)
    lane = lax.broadcasted_iota(jnp.int32, (1, 2 * NA_HEAD_DIM), 1)
    nk_win = NA_KROWS * GRID_W

    def mask(h):
        return (lane >= NA_HEAD_DIM) if (h % 2) else (lane < NA_HEAD_DIM)

    def run(win_start):
        def scores(h):
            sl = slice((h // 2) * 128, (h // 2 + 1) * 128)
            qm = jnp.where(mask(h), q_ref[0, :, sl], 0).astype(BF16)
            s_c = lax.dot_general(qm, k_ref[0, SEQ:ROWS, sl], (((1,), (1,)), ((), ())),
                                  preferred_element_type=F32)
            if win_start is None:
                return None, s_c
            kw = k_ref[0, pl.ds(win_start, nk_win), sl]
            s_w = lax.dot_general(qm, kw, (((1,), (1,)), ((), ())),
                                  preferred_element_type=F32) + bias_ref[0, h]
            return s_w, s_c

        def head_out(h, s_w, s_c):
            sl = slice((h // 2) * 128, (h // 2 + 1) * 128)
            mk = mask(h)
            m = jnp.max(s_c, axis=-1, keepdims=True)
            if s_w is not None:
                m = jnp.maximum(m, jnp.max(s_w, axis=-1, keepdims=True))
            p_c = jnp.exp2(s_c - m)
            l = jnp.sum(p_c, axis=-1, keepdims=True)
            o = jnp.dot(p_c.astype(BF16), jnp.where(mk, v_ref[0, SEQ:ROWS, sl], 0).astype(BF16),
                        preferred_element_type=F32)
            if s_w is not None:
                p_w = jnp.exp2(s_w - m)
                l = l + jnp.sum(p_w, axis=-1, keepdims=True)
                vw = v_ref[0, pl.ds(win_start, nk_win), sl]
                o = o + jnp.dot(p_w.astype(BF16), jnp.where(mk, vw, 0).astype(BF16),
                                preferred_element_type=F32)
            return o * (1.0 / l)

        s = [scores(0)] + [None] * (NA_HEADS - 1)
        outs = [None] * NA_HEADS
        for h in range(NA_HEADS):
            if h + 1 < NA_HEADS:
                s[h + 1] = scores(h + 1)
            outs[h] = head_out(h, *s[h])
            s[h] = None
            if h % 2 == 1:
                sl = slice((h // 2) * 128, (h // 2 + 1) * 128)
                o_ref[0, :, sl] = (outs[h - 1] + outs[h]).astype(BF16)

    @pl.when(g < N_LAT_TILES)
    def _():
        ks = jnp.clip(g * NA_GROUP - NA_WIN_R // 2, 0, GRID_ROWS - NA_KROWS)
        run(pl.multiple_of(ks * GRID_W, GRID_W))

    @pl.when(g == N_LAT_TILES)
    def _():
        run(None)


def _na_attn(P, bias_tab, n_tiles):
    B = P.shape[0]
    width = NA_HEADS * NA_HEAD_DIM
    nq, nk = NA_GROUP * GRID_W, NA_KROWS * GRID_W

    def bias_map(b, g):
        return (jnp.where(g == 0, 0, jnp.where(g >= N_LAT_TILES - 1, 2, 1)), 0, 0, 0)

    return pl.pallas_call(
        _na_kernel,
        grid=(B, n_tiles),
        in_specs=[pl.BlockSpec((1, TILE, width), lambda b, g: (b, g, OFF_NA_Q // width)),
                  pl.BlockSpec((1, ROWS, width), lambda b, g: (b, 0, OFF_NA_K // width)),
                  pl.BlockSpec((1, ROWS, width), lambda b, g: (b, 0, OFF_NA_V // width)),
                  pl.BlockSpec((1, NA_HEADS, nq, nk), bias_map)],
        out_specs=pl.BlockSpec((1, TILE, width), lambda b, g: (b, g, 0)),
        out_shape=jax.ShapeDtypeStruct((B, n_tiles * TILE, width), BF16),
        compiler_params=_params(("arbitrary", "arbitrary")),
        name="na_attn",
    )(P, P, P, bias_tab)


def _na_bias_table(rpb):
    onehot, col_mask, drow_blocks = _na_bias_index()
    rc = jnp.dot(rpb.astype(F32).reshape(-1, 2 * NA_WIN_C - 1), jnp.asarray(onehot),
                 precision=lax.Precision.HIGHEST)
    rc = rc.reshape(NA_HEADS, 2 * NA_WIN_R - 1, GRID_W, GRID_W) * math.log2(math.e)
    rc = jnp.where(col_mask[None, None], rc, NA_NEG)
    neg = jnp.full((NA_HEADS, GRID_W, GRID_W), NA_NEG, F32)
    pats = []
    for p in range(3):
        rows = []
        for i in range(NA_GROUP):
            rows.append(jnp.concatenate(
                [neg if d < 0 else rc[:, d] for d in drow_blocks[p][i]], axis=-1))
        pats.append(jnp.concatenate(rows, axis=1))
    return jnp.stack(pats, axis=0)


def _hy_filter_kernel(z_ref, tn_ref, dl_ref, w1_ref, b1_ref, fr_ref, w2_ref, b2_ref, w3_ref, o_ref, a_ref):
    hp = lax.Precision.HIGHEST

    @pl.when(pl.program_id(0) == 0)
    def _():
        fr = fr_ref[...]
        a = jnp.sin(fr[0:1, :] * (jnp.dot(z_ref[...], w1_ref[...], preferred_element_type=F32, precision=hp)
                                  + b1_ref[...]))
        a_ref[...] = jnp.sin(fr[1:2, :] * (jnp.dot(a, w2_ref[...], preferred_element_type=F32, precision=hp)
                                           + b2_ref[...]))

    filt = jnp.dot(a_ref[...], w3_ref[...], preferred_element_type=F32, precision=hp)
    window = jnp.exp(-tn_ref[...] * dl_ref[...])
    f0 = filt[:, 0:HY_CT] * window
    f1 = filt[:, HY_CT:] * window
    den = (jnp.sum(jnp.abs(f0), axis=0, keepdims=True) + jnp.sum(jnp.abs(f1), axis=0, keepdims=True)) + EPS
    o_ref[:, 0:HY_CT] = f0 / den
    o_ref[:, HY_CT:] = f1 / den


def _hy_filters(L, w1p, b1, freq, w2, b2, w3):
    zp, t_norm, deltas = (jnp.asarray(t) for t in _hyena_pos_features(L))
    n_ct = HY_WIDTH // HY_CT
    w3t = w3.reshape(HY_FFN, 2, 2, n_ct, HY_CT).transpose(0, 1, 3, 2, 4).reshape(HY_FFN, 4 * HY_WIDTH)
    const = lambda n: (0, 0)
    return pl.pallas_call(
        _hy_filter_kernel,
        grid=(2 * n_ct,),
        in_specs=[pl.BlockSpec((L, 128), const), pl.BlockSpec((L, 1), const),
                  pl.BlockSpec((1, HY_CT), lambda n: (0, n % n_ct)), pl.BlockSpec((128, HY_FFN), const),
                  pl.BlockSpec((1, HY_FFN), const), pl.BlockSpec((2, HY_FFN), const),
                  pl.BlockSpec((HY_FFN, HY_FFN), const), pl.BlockSpec((1, HY_FFN), const),
                  pl.BlockSpec((HY_FFN, 2 * HY_CT), lambda n: (0, n))],
        out_specs=pl.BlockSpec((L, 2 * HY_CT), lambda n: (0, n)),
        out_shape=jax.ShapeDtypeStruct((L, 4 * HY_WIDTH), F32),
        scratch_shapes=[pltpu.VMEM((L, HY_FFN), F32)],
        compiler_params=_params(("arbitrary",)),
        name="hyena_filters",
    )(zp, t_norm, deltas, w1p, b1.reshape(1, -1), freq, w2, b2.reshape(1, -1), w3t)


def _hy_spectrum_kernel(f_ref, c2_ref, s2_ref, cp_ref, sp_ref, o_ref, *, L):
    half = f_ref.shape[1] // 2
    row = lax.broadcasted_iota(jnp.int32, (L, half), 0)
    ff = f_ref[:, 0:half]
    fb = jnp.where(row == 0, 0.0, f_ref[:, half:])

    fs = (ff + fb).astype(BF16)
    fd = (ff - fb).astype(BF16)
    a_s = jnp.dot(c2_ref[...], fs, preferred_element_type=F32)
    b_s = jnp.dot(s2_ref[...], fs, preferred_element_type=F32)
    a_d = jnp.dot(c2_ref[...], fd, preferred_element_type=F32)
    b_d = jnp.dot(s2_ref[...], fd, preferred_element_type=F32)
    cp, sp = cp_ref[...], sp_ref[...]
    scale = 1.0 / L
    o_ref[0] = (cp * a_s + sp * b_s) * scale
    o_ref[1] = (sp * a_d - cp * b_d) * scale


def _hy_spectrum(filt, L):
    tabs = _dft_tables(L)
    c2, s2 = jnp.asarray(tabs['c2']).astype(BF16), jnp.asarray(tabs['s2']).astype(BF16)
    n_ct = HY_WIDTH // HY_CT
    ft = min(L, 512)
    return pl.pallas_call(
        functools.partial(_hy_spectrum_kernel, L=L),
        grid=(2 * n_ct, L // ft),
        in_specs=[pl.BlockSpec((L, 2 * HY_CT), lambda n, f: (0, n)),
                  pl.BlockSpec((ft, L), lambda n, f: (f, 0)),
                  pl.BlockSpec((ft, L), lambda n, f: (f, 0)),
                  pl.BlockSpec((ft, 1), lambda n, f: (f, 0)),
                  pl.BlockSpec((ft, 1), lambda n, f: (f, 0))],
        out_specs=pl.BlockSpec((2, ft, HY_CT), lambda n, f: (0, f, n)),
        out_shape=jax.ShapeDtypeStruct((2, L, 2 * n_ct * HY_CT), F32),
        compiler_params=_params(("arbitrary", "arbitrary")),
        name="hyena_spectrum",
    )(filt, c2, s2, jnp.asarray(tabs['cphi']), jnp.asarray(tabs['sphi']))


def _hy_conv_kernel(x1e_ref, x1o_ref, x2e_ref, x2o_ref, ze_ref, zo_ref, w_ref, b_ref, skip_ref, h_ref,
                    ce_ref, co_ref, cet_ref, cot_ref, oe_ref, oo_ref, we_ref, wo_ref, *, L):
    H2 = L // 2
    row = lax.broadcasted_iota(jnp.int32, (H2, HY_CT), 0)

    def short_conv(pe_ref, po_ref, i):
        pe, po = pe_ref[0].astype(F32), po_ref[0].astype(F32)
        w = w_ref[i]
        b = b_ref[i]
        po_prev = jnp.where(row == 0, 0.0, pltpu.roll(po, 1, 0))
        pe_next = jnp.where(row == H2 - 1, 0.0, pltpu.roll(pe, H2 - 1, 0))
        ue = po_prev * w[0:1, :] + pe * w[1:2, :] + po * w[2:3, :] + b
        uo = pe * w[0:1, :] + po * w[1:2, :] + pe_next * w[2:3, :] + b
        return ue, uo

    ft = min(H2, 512)
    ze, zo = short_conv(ze_ref, zo_ref, 2)
    for n, (ge_ref, go_ref) in enumerate(((x1e_ref, x1o_ref), (x2e_ref, x2o_ref))):
        zeb, zob = ze.astype(BF16), zo.astype(BF16)
        for i in range(H2 // ft):
            lo = slice(i * ft, (i + 1) * ft)
            hi = slice(H2 + i * ft, H2 + (i + 1) * ft)
            a_lo = jnp.dot(ce_ref[lo, :], zeb, preferred_element_type=F32)
            b_lo = jnp.dot(co_ref[lo, :], zob, preferred_element_type=F32)
            a_hi = jnp.dot(ce_ref[hi, :], zeb, preferred_element_type=F32)
            b_hi = jnp.dot(co_ref[hi, :], zob, preferred_element_type=F32)
            ur_lo, us_lo = a_lo + b_lo, a_hi - b_hi
            ur_hi, us_hi = a_hi + b_hi, a_lo - b_lo
            hre, him = h_ref[0, n, lo, :], h_ref[1, n, lo, :]
            yre_lo = ur_lo * hre + us_lo * him
            yim_lo = ur_lo * him - us_lo * hre
            hre, him = h_ref[0, n, hi, :], h_ref[1, n, hi, :]
            yre_hi = ur_hi * hre + us_hi * him
            yim_hi = ur_hi * him - us_hi * hre
            we_ref[lo, :] = (yre_lo - yim_hi).astype(BF16)
            we_ref[hi, :] = (yre_hi - yim_lo).astype(BF16)
            wo_ref[lo, :] = (yre_lo + yim_hi).astype(BF16)
            wo_ref[hi, :] = (yre_hi + yim_lo).astype(BF16)
        ye = jnp.dot(cet_ref[...], we_ref[...], preferred_element_type=F32)
        yo = jnp.dot(cot_ref[...], wo_ref[...], preferred_element_type=F32)
        ge, go = short_conv(ge_ref, go_ref, n)
        ze = ge * (ye + ze * skip_ref[n])
        zo = go * (yo + zo * skip_ref[n])
    oe_ref[0] = ze.astype(BF16)
    oo_ref[0] = zo.astype(BF16)


def _hy_conv(P, row_blk, L, conv_w, conv_b, skip, spec):
    B = P.shape[0]
    n_ct = HY_WIDTH // HY_CT
    H2 = L // 2
    tabs = _dft_tables(L)
    ce, co, cet, cot = (jnp.asarray(tabs[k]).astype(BF16) for k in ('ce', 'co', 'cet', 'cot'))
    w4 = conv_w.reshape(3, 3, n_ct, HY_CT).transpose(2, 1, 0, 3)
    b4 = conv_b.reshape(3, n_ct, 1, HY_CT).transpose(1, 0, 2, 3)
    sk4 = skip.reshape(2, n_ct, 1, HY_CT).transpose(1, 0, 2, 3)
    h5 = spec.reshape(2, L, 2, n_ct, HY_CT).transpose(3, 0, 2, 1, 4)
    P2 = P.reshape(B, ROWS // 2, 2 * P_COLS)

    def pcol(off, odd):
        base = (odd * P_COLS + off) // HY_CT
        return pl.BlockSpec((1, H2, HY_CT), lambda c, b: (b, row_blk, base + c))

    const = lambda c, b: (0, 0)
    out = pl.BlockSpec((1, H2, HY_CT), lambda c, b: (b, 0, c))
    shp = jax.ShapeDtypeStruct((B, H2, HY_WIDTH), BF16)
    ye, yo = pl.pallas_call(
        functools.partial(_hy_conv_kernel, L=L),
        grid=(n_ct, B),
        in_specs=[pcol(OFF_HY_X1, 0), pcol(OFF_HY_X1, 1), pcol(OFF_HY_X2, 0), pcol(OFF_HY_X2, 1),
                  pcol(OFF_HY_ZIN, 0), pcol(OFF_HY_ZIN, 1),
                  pl.BlockSpec((None, 3, 3, HY_CT), lambda c, b: (c, 0, 0, 0)),
                  pl.BlockSpec((None, 3, 1, HY_CT), lambda c, b: (c, 0, 0, 0)),
                  pl.BlockSpec((None, 2, 1, HY_CT), lambda c, b: (c, 0, 0, 0)),
                  pl.BlockSpec((None, 2, 2, L, HY_CT), lambda c, b: (c, 0, 0, 0, 0),
                               pipeline_mode=pl.Buffered(1)),
                  _single((L, H2), const), _single((L, H2), const),
                  _single((H2, L), const), _single((H2, L), const)],
        out_specs=[out, out],
        out_shape=[shp, shp],
        scratch_shapes=[pltpu.VMEM((L, HY_CT), BF16), pltpu.VMEM((L, HY_CT), BF16)],
        compiler_params=_params(("arbitrary", "arbitrary")),
        name="hyena_conv",
    )(P2, P2, P2, P2, P2, P2, w4, b4, sk4, h5, ce, co, cet, cot)
    return jnp.stack([ye, yo], axis=2).reshape(B, L, HY_WIDTH)


def _gla_kernel(qk_ref, v_ref, misc_ref, wg_ref, bg_ref, ng_ref, o_ref,
                qe_s, ke_s, qi_s, ko_s, dec_s, o_scr, s_scr):
    C = GLA_CHUNK
    HK = GLA_HEADS * GLA_DK
    NC = GLA_NCHUNK
    logit = jnp.dot(misc_ref[0], wg_ref[...], preferred_element_type=F32) + bg_ref[...]
    g = (jnp.minimum(logit, 0.0) - jnp.log(1.0 + jnp.exp(-jnp.abs(logit)))) * (1.0 / GLA_TAU)
    rmod = lax.broadcasted_iota(jnp.int32, (ROWS, 1), 0) % C
    p = g
    shift = 1
    while shift < C:
        p = p + jnp.where(rmod >= shift, pltpu.roll(p, shift, 0), 0.0)
        shift *= 2
    p3 = p.reshape(NC, C, 2 * HK)
    g3 = g.reshape(NC, C, 2 * HK)
    tot = p3[:, C - 1:C, :]
    q3 = (qk_ref[0, :, 0:HK].astype(F32) * (GLA_DK ** -0.5)).reshape(NC, C, HK)
    k3 = qk_ref[0, :, HK:2 * HK].astype(F32).reshape(NC, C, HK)
    for d in range(2):
        cs = slice(d * HK, (d + 1) * HK)
        if d == 0:
            bc = p3[:, :, cs]
            bm = bc[:, C // 2 - 1:C // 2, :]
        else:
            bc = tot[:, :, cs] - p3[:, :, cs] + g3[:, :, cs]
            bm = bc[:, C // 2:C // 2 + 1, :]
        bl = tot[:, :, cs]
        qe = q3 * jnp.exp(bc - bm)
        ke = k3 * jnp.exp(bm - bc)
        qe_s[d] = qe.reshape(ROWS, HK).astype(BF16)
        ke_s[d] = ke.reshape(ROWS, HK).astype(BF16)
        qi_s[d] = (qe * jnp.exp(bm)).reshape(ROWS, HK).astype(BF16)
        ko_s[d] = (ke * jnp.exp(bl - bm)).reshape(ROWS, HK).astype(BF16)
        dec_s[d] = jnp.broadcast_to(jnp.exp(bl), (NC, 8, HK)).reshape(NC * 8, HK)
    s_scr[...] = jnp.zeros_like(s_scr)

    ri = lax.broadcasted_iota(jnp.int32, (GLA_HEADS * C, C), 0) % C
    ci = lax.broadcasted_iota(jnp.int32, (GLA_HEADS * C, C), 1)
    keep = (ri >= ci, ri <= ci)
    lane = lax.broadcasted_iota(jnp.int32, (1, HK), 1)
    lane2 = lax.broadcasted_iota(jnp.int32, (1, 2 * GLA_DK), 1)

    def intra_scores(d, c):
        rows = pl.ds(pl.multiple_of(c * C, C), C)
        qe = qe_s[d, rows, :]
        stack = jnp.concatenate(
            [jnp.where((lane >= h * GLA_DK) & (lane < (h + 1) * GLA_DK), qe, 0).astype(BF16)
             for h in range(GLA_HEADS)], axis=0)
        a_all = lax.dot_general(stack, ke_s[d, rows, :], (((1,), (1,)), ((), ())),
                                preferred_element_type=F32)
        return jnp.where(keep[d], a_all, 0.0).astype(BF16)

    def state_terms(d, c):
        rows = pl.ds(pl.multiple_of(c * C, C), C)
        qi = qi_s[d, rows, :]
        ko = ko_s[d, rows, :]
        dec = dec_s[d, pl.ds(pl.multiple_of(c * 8, 8), 8), :][0:1, :]
        inter = []
        for h in range(GLA_HEADS):
            sl = slice((h // 2) * 128, (h // 2 + 1) * 128)
            mk = (lane2 >= GLA_DK) if (h % 2) else (lane2 < GLA_DK)
            v_h = v_ref[0, rows, h * GLA_DV:(h + 1) * GLA_DV]
            st = s_scr[d, h]
            qi_h = jnp.where(mk, qi[:, sl], 0).astype(BF16)
            inter.append(lax.dot_general(qi_h, st.astype(BF16), (((1,), (1,)), ((), ())),
                                         preferred_element_type=F32))
            ko_h = jnp.where(mk, ko[:, sl], 0).astype(BF16)
            kv_t = lax.dot_general(v_h, ko_h, (((0,), (0,)), ((), ())), preferred_element_type=F32)
            s_scr[d, h] = st * dec[:, sl] + kv_t
        return inter

    def write_out(d, c, a_all, inter):
        rows = pl.ds(pl.multiple_of(c * C, C), C)
        for h in range(GLA_HEADS):
            v_h = v_ref[0, rows, h * GLA_DV:(h + 1) * GLA_DV]
            o_scr[d, rows, h * GLA_DV:(h + 1) * GLA_DV] = (
                jnp.dot(a_all[h * C:(h + 1) * C], v_h, preferred_element_type=F32) + inter[h])

    def step(i, carry):
        cf = jnp.where(i < GLA_CTX_CHUNKS, GLA_NCHUNK - GLA_CTX_CHUNKS + i, i - GLA_CTX_CHUNKS)
        cb = GLA_NCHUNK - 1 - i
        a_f = intra_scores(0, cf)
        a_b = intra_scores(1, cb)
        i_f = state_terms(0, cf)
        i_b = state_terms(1, cb)
        write_out(0, cf, a_f, i_f)
        write_out(1, cb, a_b, i_b)
        return carry

    lax.fori_loop(0, GLA_NCHUNK, step, 0, unroll=2)

    ng = ng_ref[...]
    for t in range(N_TILES):
        rs = slice(t * TILE, (t + 1) * TILE)
        for h in range(GLA_HEADS):
            cs = slice(h * GLA_DV, (h + 1) * GLA_DV)
            o = o_scr[0, rs, cs] + o_scr[1, rs, cs]
            o = o * lax.rsqrt(jnp.mean(o * o, axis=-1, keepdims=True) + EPS) * ng
            o_ref[0, rs, cs] = o.astype(BF16)


def _gla(P, wg, bg, norm_g):
    B = P.shape[0]
    width = GLA_HEADS * GLA_DV
    const = lambda b: (0, 0)
    return pl.pallas_call(
        _gla_kernel,
        grid=(B,),
        in_specs=[pl.BlockSpec((1, ROWS, 512), lambda b: (b, 0, OFF_GLA_QK // 512)),
                  pl.BlockSpec((1, ROWS, width), lambda b: (b, 0, OFF_GLA_V // width)),
                  pl.BlockSpec((1, ROWS, 128), lambda b: (b, 0, (OFF_MLA + 384) // 128)),
                  pl.BlockSpec((128, 2 * GLA_HEADS * GLA_DK), const),
                  pl.BlockSpec((1, 2 * GLA_HEADS * GLA_DK), const),
                  pl.BlockSpec((1, GLA_DV), const)],
        out_specs=pl.BlockSpec((1, ROWS, width), lambda b: (b, 0, 0)),
        out_shape=jax.ShapeDtypeStruct((B, ROWS, width), BF16),
        scratch_shapes=[pltpu.VMEM((2, ROWS, GLA_HEADS * GLA_DK), BF16),
                        pltpu.VMEM((2, ROWS, GLA_HEADS * GLA_DK), BF16),
                        pltpu.VMEM((2, ROWS, GLA_HEADS * GLA_DK), BF16),
                        pltpu.VMEM((2, ROWS, GLA_HEADS * GLA_DK), BF16),
                        pltpu.VMEM((2, GLA_NCHUNK * 8, GLA_HEADS * GLA_DK), F32),
                        pltpu.VMEM((2, ROWS, width), F32),
                        pltpu.VMEM((2, GLA_HEADS, GLA_DV, 2 * GLA_DK), F32)],
        compiler_params=_params(("arbitrary",)),
        name="gla",
    )(P, P, P, wg, bg, norm_g.reshape(1, -1))


def _merge_kernel(ya_ref, yb_ref, yh_ref, yg_ref, za_ref, zb_ref, zh_ref, zg_ref, gt_ref,
                  x_ref, mod_ref, pg_ref, wb_ref, wo_ref, o_ref):
    merged = None
    one = jnp.asarray(1.0, BF16)
    for i, (y_ref, z_ref) in enumerate(((ya_ref, za_ref), (yb_ref, zb_ref), (yh_ref, zh_ref), (yg_ref, zg_ref))):
        z = z_ref[0]
        t = y_ref[0] * (z * (jnp.tanh(z) + one))
        br = jnp.dot(t, wb_ref[i], preferred_element_type=F32)
        gl = gt_ref[0, :, i * D_MODEL:(i + 1) * D_MODEL].astype(F32)
        term = br * (jnp.tanh(gl) + 1.0)
        merged = term if merged is None else merged + term
    out = jnp.dot(merged.astype(BF16), wo_ref[...], preferred_element_type=F32)
    xn = out * lax.rsqrt(jnp.mean(out * out, axis=-1, keepdims=True) + EPS) * pg_ref[...]
    gate = mod_ref[0][:, 2 * D_MODEL:3 * D_MODEL]
    o_ref[0] = x_ref[0] + gate * xn


def _merge(x, P, ya, yb, yh, yg, mod3, mod_row, post_g, wb, wo, row_off):
    B, R, _ = x.shape
    n_t = R // TILE
    const2 = lambda b, t: (0, 0)

    def uni(width, off):
        return pl.BlockSpec((1, TILE, width), lambda b, t: (b, t + row_off, off // width))

    return pl.pallas_call(
        _merge_kernel,
        grid=(B, n_t),
        in_specs=[uni(512, 0), uni(512, 0),
                  pl.BlockSpec((1, TILE, 512), lambda b, t: (b, t, 0)),
                  uni(512, 0),
                  uni(512, OFF_MLA_Z), uni(512, OFF_NA_Z), uni(512, OFF_HY_Z), uni(512, OFF_GLA_Z),
                  uni(4 * D_MODEL, OFF_MERGE),
                  pl.BlockSpec((1, TILE, D_MODEL), lambda b, t: (b, t, 0)),
                  pl.BlockSpec((1, 1, 3 * D_MODEL), lambda b, t: (mod_row(b), 0, 0)),
                  pl.BlockSpec((1, D_MODEL), const2),
                  pl.BlockSpec((4, 512, D_MODEL), lambda b, t: (0, 0, 0)),
                  pl.BlockSpec((D_MODEL, D_MODEL), const2)],
        out_specs=pl.BlockSpec((1, TILE, D_MODEL), lambda b, t: (b, t, 0)),
        out_shape=jax.ShapeDtypeStruct((B, R, D_MODEL), F32),
        compiler_params=_params(("arbitrary", "arbitrary")),
        name="merge",
    )(ya, yb, yh, yg, P, P, P, P, P, x, mod3, post_g.reshape(1, -1), wb, wo)


def _build_w_big(w):
    def cols(name):
        a, b = _ref_cols(name)
        return w[:, a:b]

    kr = cols('mla_kr')
    krx = kr[:, _ROPE_PERM] * _ROPE_SIGN
    pad32 = jnp.zeros((D_MODEL, 32), w.dtype)
    na = cols('na_qkv')
    nw = NA_HEADS * NA_HEAD_DIM
    na = jnp.concatenate([na[:, :nw] * (NA_HEAD_DIM ** -0.5 * math.log2(math.e)), na[:, nw:]], axis=1)
    parts = [0.5 * cols('merge'), cols('mla_cq'), cols('mla_ckv'), kr, cols('gla_glr'), krx, pad32,
             0.5 * cols('mla_z'), na, 0.5 * cols('na_z'), cols('hy_proj'), 0.5 * cols('hy_z'),
             cols('gla_qk'), cols('gla_v'), 0.5 * cols('gla_z')]
    big = jnp.concatenate(parts, axis=1).astype(BF16)
    assert big.shape == (D_MODEL, P_COLS)
    return big


def _build_mla_weights(q_norm, w_uq, kv_norm, w_ukv):
    dq = MLA_NOPE + MLA_ROPE
    wq3 = w_uq.reshape(MLA_Q_RANK, MLA_HEADS, dq)
    rot = wq3[:, :, MLA_NOPE:]
    rotx = rot[:, :, _ROPE_PERM] * _ROPE_SIGN
    wq = jnp.concatenate([wq3, rotx], axis=-1).reshape(MLA_Q_RANK, MLA_HEADS * MLA_SLOT)
    wkv3 = w_ukv.reshape(MLA_KV_RANK, MLA_HEADS, MLA_NOPE + MLA_V)
    zk = jnp.zeros((MLA_KV_RANK, MLA_HEADS, MLA_SLOT - MLA_NOPE), F32)
    wk = jnp.concatenate([wkv3[:, :, :MLA_NOPE], zk], axis=-1).reshape(MLA_KV_RANK, MLA_HEADS * MLA_SLOT)
    wv3 = wkv3[:, :, MLA_NOPE:]
    zv = jnp.zeros_like(wv3)
    even = jnp.concatenate([wv3, zv], axis=-1)
    odd = jnp.concatenate([zv, wv3], axis=-1)
    sel = (np.arange(MLA_HEADS) % 2 == 1)[None, :, None]
    wv = jnp.where(sel, odd, even).reshape(MLA_KV_RANK, MLA_HEADS * MLA_SLOT)
    e1 = np.zeros((128, MLA_SLOT), np.float32)
    for i in range(MLA_ROPE):
        e1[i, MLA_NOPE + i] = 1.0
        e1[64 + i, MLA_NOPE + MLA_ROPE + i] = 1.0
    return dict(q_norm=q_norm.reshape(1, -1), kv_norm=kv_norm.reshape(1, -1),
                wq=wq.astype(BF16), wk=wk.astype(BF16), wv=wv.astype(BF16),
                e1=jnp.asarray(e1, BF16))


def _build_gla_gate_weights(wg2, bg):
    hk = GLA_HEADS * GLA_DK
    wg = jnp.zeros((128, 2 * hk), F32)
    for i in range(2):
        wg = wg.at[32 + i * GLA_RANK:32 + (i + 1) * GLA_RANK, i * hk:(i + 1) * hk].set(wg2[i])
    return wg.astype(BF16), bg.reshape(1, 2 * hk)


def kernel(x, c, ctx, c_ctx, ada_w, ada_b, pre_g, post_g, w_in, mla_q_norm, mla_w_uq, mla_kv_norm, mla_w_ukv, na_rpb, hy_conv_w, hy_conv_b, hy_pe_w1, hy_pe_b1, hy_pe_freq, hy_pe_w2, hy_pe_b2, hy_pe_w3, hy_skip, gla_wg2, gla_bg, gla_norm, w_branch, w_out):
    B = x.shape[0]
    depth = ada_w.shape[0]
    mod_rows = -(-(B + 1) // 8) * 8
    cc = jnp.zeros((mod_rows, D_MODEL), F32).at[:B].set(c).at[B].set(c_ctx)
    cx = ctx
    for l in range(depth):
        need_ctx = l < depth - 1
        n_tiles = N_TILES if need_ctx else N_LAT_TILES
        mod3 = _adaln(cc, ada_w[l], ada_b[l]).reshape(mod_rows, 1, 3 * D_MODEL)
        P = _inproj(x, cx, mod3, pre_g[l], _build_w_big(w_in[l]))

        mla_w = _build_mla_weights(mla_q_norm[l], mla_w_uq[l], mla_kv_norm[l], mla_w_ukv[l])
        q, k, v = _mla_prep(P, mla_w, N_TILES)
        ya = _mla_attn(q, k, v, n_tiles)

        yb = _na_attn(P, _na_bias_table(na_rpb[l]), n_tiles)

        w1p = jnp.zeros((128, HY_FFN), F32).at[:HY_EMB].set(hy_pe_w1[l])

        def hyena(L, row_blk):
            filt = _hy_filters(L, w1p, hy_pe_b1[l], hy_pe_freq[l], hy_pe_w2[l], hy_pe_b2[l], hy_pe_w3[l])
            spec = _hy_spectrum(filt, L)
            return _hy_conv(P, row_blk, L, hy_conv_w[l], hy_conv_b[l], hy_skip[l], spec)

        yh = hyena(SEQ, 0)

        wg, bgp = _build_gla_gate_weights(gla_wg2[l], gla_bg[l])
        yg = _gla(P, wg, bgp, gla_norm[l])

        wb = (0.5 * w_branch[l]).astype(BF16)
        wo = w_out[l].astype(BF16)
        if need_ctx:
            yhc = hyena(CTX_LEN, SEQ // CTX_LEN)
            cx = _merge(cx, P, ya, yb, yhc, yg, mod3, lambda b: B, post_g[l], wb, wo, N_LAT_TILES)
        x = _merge(x, P, ya, yb, yh, yg, mod3, lambda b: b, post_g[l], wb, wo, 0)
    return x
```

```python
import functools
import math

import numpy as np
import jax
import jax.numpy as jnp
from jax import lax
from jax.experimental import pallas as pl
from jax.experimental.pallas import tpu as pltpu

F32 = jnp.float32
BF16 = jnp.bfloat16

D_MODEL = 1024
SEQ = 2048
CTX_LEN = 256
ROWS = SEQ + CTX_LEN
GRID_W = 64
GRID_ROWS = SEQ // GRID_W
EPS = 1e-6
ROPE_BASE = 10000.0
TILE = 256
N_LAT_TILES = SEQ // TILE
N_TILES = ROWS // TILE

MLA_HEADS = 8
MLA_NOPE = 64
MLA_ROPE = 32
MLA_V = 64
MLA_Q_RANK = 256
MLA_KV_RANK = 128
MLA_SLOT = 128
MLA_HEADS_PER_STEP = 4
MLA_LOOKAHEAD = 2

NA_HEADS = 8
NA_HEAD_DIM = 64
NA_WIN_R = 8
NA_WIN_C = 16
NA_GROUP = 4
NA_KROWS = NA_GROUP + NA_WIN_R - 1
NA_NEG = -1e30

HY_WIDTH = 512
HY_EMB = 33
HY_FFN = 64
HY_TARGET = 1e-2
HY_FAST_DECAY_PCT = 0.3
HY_SLOW_DECAY_PCT = 1.5
HY_CT = 256

GLA_HEADS = 4
GLA_DK = 64
GLA_DV = 128
GLA_RANK = 16
GLA_TAU = 16.0
GLA_CHUNK = 64
GLA_NCHUNK = ROWS // GLA_CHUNK
GLA_CTX_CHUNKS = CTX_LEN // GLA_CHUNK

OFF_MERGE = 0
OFF_MLA = 4096
OFF_MLA_Z = 4608
OFF_NA_Q = 5120
OFF_NA_K = 5632
OFF_NA_V = 6144
OFF_NA_Z = 6656
OFF_HY_X1 = 7168
OFF_HY_X2 = 7680
OFF_HY_ZIN = 8192
OFF_HY_Z = 8704
OFF_GLA_QK = 9216
OFF_GLA_V = 9728
OFF_GLA_Z = 10240
P_COLS = 10752
INPROJ_TN = 512

VMEM_LIMIT = 56 * 1024 * 1024

_REF_LAYOUT = (
    ('mla_cq', 256), ('mla_ckv', 128), ('mla_kr', 32), ('mla_z', 512), ('na_qkv', 1536),
    ('na_z', 512), ('hy_proj', 1536), ('hy_z', 512), ('gla_qk', 512), ('gla_v', 512),
    ('gla_glr', 32), ('gla_z', 512), ('merge', 4096),
)


def _ref_cols(name):
    start = 0
    for n, w in _REF_LAYOUT:
        if n == name:
            return start, start + w
        start += w
    raise KeyError(name)


def _params(sem, vmem=VMEM_LIMIT):
    return pltpu.CompilerParams(dimension_semantics=sem, vmem_limit_bytes=vmem)


def _single(shape, index_map):
    return pl.BlockSpec(shape, index_map, pipeline_mode=pl.Buffered(1))


_ROPE_PERM = np.concatenate([np.arange(8, 16), np.arange(0, 8), np.arange(24, 32), np.arange(16, 24)])
_ROPE_SIGN = np.concatenate([-np.ones(8), np.ones(8), -np.ones(8), np.ones(8)]).astype(np.float32)


@functools.lru_cache(maxsize=None)
def _rope_tables():
    t = np.arange(SEQ)
    row = (t // GRID_W).astype(np.float32)
    col = (t % GRID_W).astype(np.float32)
    half = MLA_ROPE // 2
    inv = np.power(np.float32(ROPE_BASE), -np.arange(0, half, 2, dtype=np.float32) / np.float32(half))
    ar, ac = row[:, None] * inv, col[:, None] * inv
    ang = np.concatenate([ar, ar, ac, ac], axis=-1).astype(np.float32)
    cos = np.ones((ROWS, MLA_ROPE), np.float32)
    sin = np.zeros((ROWS, MLA_ROPE), np.float32)
    cos[:SEQ] = np.cos(ang)
    sin[:SEQ] = np.sin(ang)
    scale = np.float32((MLA_NOPE + MLA_ROPE) ** -0.5 * math.log2(math.e))
    cos_q = np.zeros((ROWS, MLA_SLOT), np.float32)
    sin_q = np.zeros((ROWS, MLA_SLOT), np.float32)
    cos_q[:, :MLA_NOPE] = scale
    cos_q[:, MLA_NOPE:MLA_NOPE + MLA_ROPE] = cos * scale
    sin_q[:, MLA_NOPE:MLA_NOPE + MLA_ROPE] = sin * scale
    cos_k = np.zeros((ROWS, MLA_SLOT), np.float32)
    sin_k = np.zeros((ROWS, MLA_SLOT), np.float32)
    cos_k[:, MLA_NOPE:MLA_NOPE + MLA_ROPE] = cos
    sin_k[:, MLA_NOPE:MLA_NOPE + MLA_ROPE] = sin
    return cos_q, sin_q, cos_k, sin_k


@functools.lru_cache(maxsize=None)
def _dft_tables(L):
    n = 2 * L
    k = np.arange(L, dtype=np.int64)
    freq = np.concatenate([k[:L // 2], L - 1 - k[:L // 2]])
    prod = np.mod(np.outer(2 * freq + 1, 2 * k + 1), 4 * n)
    ang = prod.astype(np.float64) * (math.pi / (2 * n))
    c2 = np.cos(ang).astype(np.float32)
    s2 = np.sin(ang).astype(np.float32)
    phi = (freq.astype(np.float64) + 0.5) * (math.pi / n)
    cphi = np.cos(phi).astype(np.float32)[:, None]
    sphi = np.sin(phi).astype(np.float32)[:, None]
    ce, co = c2[:, 0::2], c2[:, 1::2]
    return dict(c2=c2, s2=s2, cphi=cphi, sphi=sphi, ce=np.ascontiguousarray(ce), co=np.ascontiguousarray(co),
                cet=np.ascontiguousarray(ce.T), cot=np.ascontiguousarray(co.T))


@functools.lru_cache(maxsize=None)
def _hyena_pos_features(L):
    f32 = np.float32
    t = np.arange(L, dtype=f32)
    t_norm = t / f32(max(L - 1, 1))
    bands = (HY_EMB - 1) // 2
    fr = np.linspace(1e-4, bands - 1, bands, dtype=f32)
    ang = (f32(2.0 * math.pi / L) * t[:, None] * fr[None, :]).astype(f32)
    z = np.concatenate([t_norm[:, None], np.cos(ang), -np.sin(ang)], axis=-1).astype(f32)
    zp = np.zeros((L, 128), f32)
    zp[:, :HY_EMB] = z
    max_decay = math.log(HY_TARGET) / HY_FAST_DECAY_PCT
    min_decay = math.log(HY_TARGET) / HY_SLOW_DECAY_PCT
    deltas = np.abs(np.linspace(min_decay, max_decay, HY_WIDTH, dtype=f32)).astype(f32)
    return zp, t_norm[:, None].astype(f32), deltas[None, :]


@functools.lru_cache(maxsize=None)
def _na_bias_index():
    wr = NA_WIN_R
    j = np.arange(GRID_W)
    c0 = np.clip(j - NA_WIN_C // 2, 0, GRID_W - NA_WIN_C)
    col_mask = (j[None, :] >= c0[:, None]) & (j[None, :] < c0[:, None] + NA_WIN_C)
    dcol = np.clip(j[None, :] - j[:, None], -(NA_WIN_C - 1), NA_WIN_C - 1) + NA_WIN_C - 1
    onehot = np.zeros((2 * NA_WIN_C - 1, GRID_W * GRID_W), np.float32)
    onehot[dcol.reshape(-1), np.arange(GRID_W * GRID_W)] = 1.0
    blocks = []
    for ra in (0, NA_GROUP, GRID_ROWS - NA_GROUP):
        ks = int(np.clip(ra - wr // 2, 0, GRID_ROWS - NA_KROWS))
        per_row = []
        for i in range(NA_GROUP):
            r = ra + i
            r0 = int(np.clip(r - wr // 2, 0, GRID_ROWS - wr))
            per_row.append(tuple((ks + jj - r + wr - 1) if (r0 <= ks + jj < r0 + wr) else -1
                                 for jj in range(NA_KROWS)))
        blocks.append(tuple(per_row))
    return onehot, col_mask, tuple(blocks)


def _adaln_kernel(c_ref, w_ref, b_ref, o_ref):
    c = c_ref[...]
    a = c * (1.0 / (1.0 + jnp.exp(-c)))
    o_ref[...] = jnp.dot(a, w_ref[...], preferred_element_type=F32,
                         precision=lax.Precision.HIGHEST) + b_ref[...]


def _adaln(cc, w, b):
    rows = cc.shape[0]
    tn = 768
    return pl.pallas_call(
        _adaln_kernel,
        grid=(3 * D_MODEL // tn,),
        in_specs=[pl.BlockSpec((rows, D_MODEL), lambda n: (0, 0)),
                  pl.BlockSpec((D_MODEL, tn), lambda n: (0, n)),
                  pl.BlockSpec((1, tn), lambda n: (0, n))],
        out_specs=pl.BlockSpec((rows, tn), lambda n: (0, n)),
        out_shape=jax.ShapeDtypeStruct((rows, 3 * D_MODEL), F32),
        compiler_params=_params(("arbitrary",)),
        name="adaln",
    )(cc, w, b.reshape(1, -1))


def _inproj_kernel(x_ref, cx_ref, modl_ref, modc_ref, g_ref, w_ref, o_ref, h_ref):
    @pl.when(pl.program_id(1) == 0)
    def _():
        g = g_ref[...]

        def norm_mod(xv, mod):
            ms = jnp.mean(xv * xv, axis=-1, keepdims=True)
            xn = xv * lax.rsqrt(ms + EPS) * g
            return (xn * (1.0 + mod[:, D_MODEL:2 * D_MODEL]) + mod[:, 0:D_MODEL]).astype(BF16)

        modl = modl_ref[0]
        for i in range(N_LAT_TILES):
            h_ref[i * TILE:(i + 1) * TILE, :] = norm_mod(x_ref[0, i * TILE:(i + 1) * TILE, :], modl)
        h_ref[SEQ:ROWS, :] = norm_mod(cx_ref[0], modc_ref[0])

    o_ref[0] = jnp.dot(h_ref[...], w_ref[...], preferred_element_type=F32).astype(BF16)


def _inproj(x, cx, mod3, pre_g, w_big):
    B = x.shape[0]
    return pl.pallas_call(
        _inproj_kernel,
        grid=(B, P_COLS // INPROJ_TN),
        in_specs=[pl.BlockSpec((1, SEQ, D_MODEL), lambda b, n: (b, 0, 0)),
                  pl.BlockSpec((1, CTX_LEN, D_MODEL), lambda b, n: (b, 0, 0)),
                  pl.BlockSpec((1, 1, 3 * D_MODEL), lambda b, n: (b, 0, 0)),
                  pl.BlockSpec((1, 1, 3 * D_MODEL), lambda b, n: (B, 0, 0)),
                  pl.BlockSpec((1, D_MODEL), lambda b, n: (0, 0)),
                  pl.BlockSpec((D_MODEL, INPROJ_TN), lambda b, n: (0, n))],
        out_specs=pl.BlockSpec((1, ROWS, INPROJ_TN), lambda b, n: (b, 0, n)),
        out_shape=jax.ShapeDtypeStruct((B, ROWS, P_COLS), BF16),
        scratch_shapes=[pltpu.VMEM((ROWS, D_MODEL), BF16)],
        compiler_params=_params(("arbitrary", "arbitrary")),
        name="inproj",
    )(x, cx, mod3, mod3, pre_g.reshape(1, -1), w_big)


def _mla_prep_kernel(p_ref, qn_ref, kvn_ref, wq_ref, wk_ref, wv_ref, e1_ref,
                     cq_ref, sq_ref, ck_ref, sk_ref, q_ref, k_ref, v_ref):
    pm = p_ref[0]
    cq = pm[:, 0:MLA_Q_RANK].astype(F32)
    ckv = pm[:, MLA_Q_RANK:MLA_Q_RANK + MLA_KV_RANK].astype(F32)
    misc = pm[:, MLA_Q_RANK + MLA_KV_RANK:]
    cqn = (cq * lax.rsqrt(jnp.mean(cq * cq, axis=-1, keepdims=True) + EPS) * qn_ref[...]).astype(BF16)
    ckvn = (ckv * lax.rsqrt(jnp.mean(ckv * ckv, axis=-1, keepdims=True) + EPS) * kvn_ref[...]).astype(BF16)
    q = jnp.dot(cqn, wq_ref[...], preferred_element_type=F32)
    qx = pltpu.roll(q, MLA_HEADS * MLA_SLOT - MLA_ROPE, 1)
    kn = jnp.dot(ckvn, wk_ref[...], preferred_element_type=F32)
    v = jnp.dot(ckvn, wv_ref[...], preferred_element_type=F32)
    kr = jnp.dot(misc, e1_ref[...], preferred_element_type=F32)
    krot = kr * ck_ref[...] + pltpu.roll(kr, MLA_SLOT - MLA_ROPE, 1) * sk_ref[...]
    cosq, sinq = cq_ref[...], sq_ref[...]
    for h in range(MLA_HEADS):
        sl = slice(h * MLA_SLOT, (h + 1) * MLA_SLOT)
        q_ref[0, :, sl] = (q[:, sl] * cosq + qx[:, sl] * sinq).astype(BF16)
        k_ref[0, :, sl] = (kn[:, sl] + krot).astype(BF16)
    lane = lax.broadcasted_iota(jnp.int32, (1, MLA_HEADS * MLA_SLOT), 1)
    upper = (lane % MLA_SLOT) >= MLA_V
    odd = (lane // MLA_SLOT) % 2 == 1
    v_ref[0] = jnp.where(upper != odd, 1.0, v).astype(BF16)


def _mla_prep(P, wts, n_tiles):
    B = P.shape[0]
    width = MLA_HEADS * MLA_SLOT
    cos_q, sin_q, cos_k, sin_k = (jnp.asarray(t) for t in _rope_tables())
    const = lambda b, t: (0, 0)
    tab = pl.BlockSpec((TILE, MLA_SLOT), lambda b, t: (t, 0))
    out = pl.BlockSpec((1, TILE, width), lambda b, t: (b, t, 0))
    shp = jax.ShapeDtypeStruct((B, ROWS, width), BF16)
    return pl.pallas_call(
        _mla_prep_kernel,
        grid=(B, N_TILES),
        in_specs=[pl.BlockSpec((1, TILE, 512), lambda b, t: (b, t, OFF_MLA // 512)),
                  pl.BlockSpec((1, MLA_Q_RANK), const),
                  pl.BlockSpec((1, MLA_KV_RANK), const),
                  pl.BlockSpec((MLA_Q_RANK, width), const),
                  pl.BlockSpec((MLA_KV_RANK, width), const),
                  pl.BlockSpec((MLA_KV_RANK, width), const),
                  pl.BlockSpec((128, MLA_SLOT), const),
                  tab, tab, tab, tab],
        out_specs=[out, out, out],
        out_shape=[shp, shp, shp],
        compiler_params=_params(("arbitrary", "arbitrary")),
        name="mla_prep",
    )(P, wts['q_norm'], wts['kv_norm'], wts['wq'], wts['wk'], wts['wv'],
      wts['e1'], cos_q, sin_q, cos_k, sin_k)


def _mla_attn_kernel(q_ref, k_ref, v_ref, o_ref):
    lane = lax.broadcasted_iota(jnp.int32, (1, MLA_SLOT), 1)

    def attend(k0, nk):
        def scores(h):
            sl = slice(h * MLA_SLOT, (h + 1) * MLA_SLOT)
            return lax.dot_general(q_ref[0, :, sl], k_ref[0, k0:k0 + nk, sl], (((1,), (1,)), ((), ())),
                                   preferred_element_type=F32)

        def head_out(h, s):
            sl = slice(h * MLA_SLOT, (h + 1) * MLA_SLOT)
            m = jnp.max(s, axis=-1, keepdims=True)
            p = jnp.exp2(s - m).astype(BF16)
            o = jnp.dot(p, v_ref[0, k0:k0 + nk, sl], preferred_element_type=F32)
            r = pltpu.roll(o, MLA_V, 1)
            keep = (lane >= MLA_V) if (h % 2) else (lane < MLA_V)
            return jnp.where(keep, o * (1.0 / r), 0.0)

        nh = MLA_HEADS_PER_STEP
        s = [scores(h) if h < MLA_LOOKAHEAD else None for h in range(nh)]
        outs = [None] * nh
        for h in range(nh):
            if h + MLA_LOOKAHEAD < nh:
                s[h + MLA_LOOKAHEAD] = scores(h + MLA_LOOKAHEAD)
            outs[h] = head_out(h, s[h])
            s[h] = None
            if h % 2 == 1:
                pr = h // 2
                o_ref[0, :, pr * MLA_SLOT:(pr + 1) * MLA_SLOT] = (outs[h - 1] + outs[h]).astype(BF16)

    is_ctx = pl.program_id(2) == N_LAT_TILES

    @pl.when(jnp.logical_not(is_ctx))
    def _():
        attend(0, ROWS)

    @pl.when(is_ctx)
    def _():
        attend(SEQ, CTX_LEN)


def _mla_attn(q, k, v, n_tiles):
    B = q.shape[0]
    width = MLA_HEADS_PER_STEP * MLA_SLOT
    return pl.pallas_call(
        _mla_attn_kernel,
        grid=(B, MLA_HEADS // MLA_HEADS_PER_STEP, n_tiles),
        in_specs=[pl.BlockSpec((1, TILE, width), lambda b, p, t: (b, t, p)),
                  pl.BlockSpec((1, ROWS, width), lambda b, p, t: (b, 0, p)),
                  pl.BlockSpec((1, ROWS, width), lambda b, p, t: (b, 0, p))],
        out_specs=pl.BlockSpec((1, TILE, width // 2), lambda b, p, t: (b, t, p)),
        out_shape=jax.ShapeDtypeStruct((B, n_tiles * TILE, MLA_HEADS * MLA_V), BF16),
        compiler_params=_params(("arbitrary", "arbitrary", "arbitrary")),
        name="mla_attn",
    )(q, k, v)


def _na_kernel(q_ref, k_ref, v_ref, bias_ref, o_ref):
    g = pl.program_id(1)
    lane = lax.broadcasted_iota(jnp.int32, (1, 2 * NA_HEAD_DIM), 1)
    nk_win = NA_KROWS * GRID_W

    def mask(h):
        return (lane >= NA_HEAD_DIM) if (h % 2) else (lane < NA_HEAD_DIM)

    def run(win_start):
        def scores(h):
            sl = slice((h // 2) * 128, (h // 2 + 1) * 128)
            qm = jnp.where(mask(h), q_ref[0, :, sl], 0).astype(BF16)
            s_c = lax.dot_general(qm, k_ref[0, SEQ:ROWS, sl], (((1,), (1,)), ((), ())),
                                  preferred_element_type=F32)
            if win_start is None:
                return None, s_c
            kw = k_ref[0, pl.ds(win_start, nk_win), sl]
            s_w = lax.dot_general(qm, kw, (((1,), (1,)), ((), ())),
                                  preferred_element_type=F32) + bias_ref[0, h]
            return s_w, s_c

        def head_out(h, s_w, s_c):
            sl = slice((h // 2) * 128, (h // 2 + 1) * 128)
            mk = mask(h)
            m = jnp.max(s_c, axis=-1, keepdims=True)
            if s_w is not None:
                m = jnp.maximum(m, jnp.max(s_w, axis=-1, keepdims=True))
            p_c = jnp.exp2(s_c - m)
            l = jnp.sum(p_c, axis=-1, keepdims=True)
            o = jnp.dot(p_c.astype(BF16), jnp.where(mk, v_ref[0, SEQ:ROWS, sl], 0).astype(BF16),
                        preferred_element_type=F32)
            if s_w is not None:
                p_w = jnp.exp2(s_w - m)
                l = l + jnp.sum(p_w, axis=-1, keepdims=True)
                vw = v_ref[0, pl.ds(win_start, nk_win), sl]
                o = o + jnp.dot(p_w.astype(BF16), jnp.where(mk, vw, 0).astype(BF16),
                                preferred_element_type=F32)
            return o * (1.0 / l)

        s = [scores(0)] + [None] * (NA_HEADS - 1)
        outs = [None] * NA_HEADS
        for h in range(NA_HEADS):
            if h + 1 < NA_HEADS:
                s[h + 1] = scores(h + 1)
            outs[h] = head_out(h, *s[h])
            s[h] = None
            if h % 2 == 1:
                sl = slice((h // 2) * 128, (h // 2 + 1) * 128)
                o_ref[0, :, sl] = (outs[h - 1] + outs[h]).astype(BF16)

    @pl.when(g < N_LAT_TILES)
    def _():
        ks = jnp.clip(g * NA_GROUP - NA_WIN_R // 2, 0, GRID_ROWS - NA_KROWS)
        run(pl.multiple_of(ks * GRID_W, GRID_W))

    @pl.when(g == N_LAT_TILES)
    def _():
        run(None)


def _na_attn(P, bias_tab, n_tiles):
    B = P.shape[0]
    width = NA_HEADS * NA_HEAD_DIM
    nq, nk = NA_GROUP * GRID_W, NA_KROWS * GRID_W

    def bias_map(b, g):
        return (jnp.where(g == 0, 0, jnp.where(g >= N_LAT_TILES - 1, 2, 1)), 0, 0, 0)

    return pl.pallas_call(
        _na_kernel,
        grid=(B, n_tiles),
        in_specs=[pl.BlockSpec((1, TILE, width), lambda b, g: (b, g, OFF_NA_Q // width)),
                  pl.BlockSpec((1, ROWS, width), lambda b, g: (b, 0, OFF_NA_K // width)),
                  pl.BlockSpec((1, ROWS, width), lambda b, g: (b, 0, OFF_NA_V // width)),
                  pl.BlockSpec((1, NA_HEADS, nq, nk), bias_map)],
        out_specs=pl.BlockSpec((1, TILE, width), lambda b, g: (b, g, 0)),
        out_shape=jax.ShapeDtypeStruct((B, n_tiles * TILE, width), BF16),
        compiler_params=_params(("arbitrary", "arbitrary")),
        name="na_attn",
    )(P, P, P, bias_tab)


def _na_bias_table(rpb):
    onehot, col_mask, drow_blocks = _na_bias_index()
    rc = jnp.dot(rpb.astype(F32).reshape(-1, 2 * NA_WIN_C - 1), jnp.asarray(onehot),
                 precision=lax.Precision.HIGHEST)
    rc = rc.reshape(NA_HEADS, 2 * NA_WIN_R - 1, GRID_W, GRID_W) * math.log2(math.e)
    rc = jnp.where(col_mask[None, None], rc, NA_NEG)
    neg = jnp.full((NA_HEADS, GRID_W, GRID_W), NA_NEG, F32)
    pats = []
    for p in range(3):
        rows = []
        for i in range(NA_GROUP):
            rows.append(jnp.concatenate(
                [neg if d < 0 else rc[:, d] for d in drow_blocks[p][i]], axis=-1))
        pats.append(jnp.concatenate(rows, axis=1))
    return jnp.stack(pats, axis=0)


def _hy_filter_kernel(z_ref, tn_ref, dl_ref, w1_ref, b1_ref, fr_ref, w2_ref, b2_ref, w3_ref, o_ref, a_ref):
    hp = lax.Precision.HIGHEST

    @pl.when(pl.program_id(0) == 0)
    def _():
        fr = fr_ref[...]
        a = jnp.sin(fr[0:1, :] * (jnp.dot(z_ref[...], w1_ref[...], preferred_element_type=F32, precision=hp)
                                  + b1_ref[...]))
        a_ref[...] = jnp.sin(fr[1:2, :] * (jnp.dot(a, w2_ref[...], preferred_element_type=F32, precision=hp)
                                           + b2_ref[...]))

    filt = jnp.dot(a_ref[...], w3_ref[...], preferred_element_type=F32, precision=hp)
    window = jnp.exp(-tn_ref[...] * dl_ref[...])
    f0 = filt[:, 0:HY_CT] * window
    f1 = filt[:, HY_CT:] * window
    den = (jnp.sum(jnp.abs(f0), axis=0, keepdims=True) + jnp.sum(jnp.abs(f1), axis=0, keepdims=True)) + EPS
    o_ref[:, 0:HY_CT] = f0 / den
    o_ref[:, HY_CT:] = f1 / den


def _hy_filters(L, w1p, b1, freq, w2, b2, w3):
    zp, t_norm, deltas = (jnp.asarray(t) for t in _hyena_pos_features(L))
    n_ct = HY_WIDTH // HY_CT
    w3t = w3.reshape(HY_FFN, 2, 2, n_ct, HY_CT).transpose(0, 1, 3, 2, 4).reshape(HY_FFN, 4 * HY_WIDTH)
    const = lambda n: (0, 0)
    return pl.pallas_call(
        _hy_filter_kernel,
        grid=(2 * n_ct,),
        in_specs=[pl.BlockSpec((L, 128), const), pl.BlockSpec((L, 1), const),
                  pl.BlockSpec((1, HY_CT), lambda n: (0, n % n_ct)), pl.BlockSpec((128, HY_FFN), const),
                  pl.BlockSpec((1, HY_FFN), const), pl.BlockSpec((2, HY_FFN), const),
                  pl.BlockSpec((HY_FFN, HY_FFN), const), pl.BlockSpec((1, HY_FFN), const),
                  pl.BlockSpec((HY_FFN, 2 * HY_CT), lambda n: (0, n))],
        out_specs=pl.BlockSpec((L, 2 * HY_CT), lambda n: (0, n)),
        out_shape=jax.ShapeDtypeStruct((L, 4 * HY_WIDTH), F32),
        scratch_shapes=[pltpu.VMEM((L, HY_FFN), F32)],
        compiler_params=_params(("arbitrary",)),
        name="hyena_filters",
    )(zp, t_norm, deltas, w1p, b1.reshape(1, -1), freq, w2, b2.reshape(1, -1), w3t)


def _hy_spectrum_kernel(f_ref, c2_ref, s2_ref, cp_ref, sp_ref, o_ref, *, L):
    half = f_ref.shape[1] // 2
    row = lax.broadcasted_iota(jnp.int32, (L, half), 0)
    ff = f_ref[:, 0:half]
    fb = jnp.where(row == 0, 0.0, f_ref[:, half:])

    fs = (ff + fb).astype(BF16)
    fd = (ff - fb).astype(BF16)
    a_s = jnp.dot(c2_ref[...], fs, preferred_element_type=F32)
    b_s = jnp.dot(s2_ref[...], fs, preferred_element_type=F32)
    a_d = jnp.dot(c2_ref[...], fd, preferred_element_type=F32)
    b_d = jnp.dot(s2_ref[...], fd, preferred_element_type=F32)
    cp, sp = cp_ref[...], sp_ref[...]
    scale = 1.0 / L
    o_ref[0] = (cp * a_s + sp * b_s) * scale
    o_ref[1] = (sp * a_d - cp * b_d) * scale


def _hy_spectrum(filt, L):
    tabs = _dft_tables(L)
    c2, s2 = jnp.asarray(tabs['c2']).astype(BF16), jnp.asarray(tabs['s2']).astype(BF16)
    n_ct = HY_WIDTH // HY_CT
    ft = min(L, 512)
    return pl.pallas_call(
        functools.partial(_hy_spectrum_kernel, L=L),
        grid=(2 * n_ct, L // ft),
        in_specs=[pl.BlockSpec((L, 2 * HY_CT), lambda n, f: (0, n)),
                  pl.BlockSpec((ft, L), lambda n, f: (f, 0)),
                  pl.BlockSpec((ft, L), lambda n, f: (f, 0)),
                  pl.BlockSpec((ft, 1), lambda n, f: (f, 0)),
                  pl.BlockSpec((ft, 1), lambda n, f: (f, 0))],
        out_specs=pl.BlockSpec((2, ft, HY_CT), lambda n, f: (0, f, n)),
        out_shape=jax.ShapeDtypeStruct((2, L, 2 * n_ct * HY_CT), F32),
        compiler_params=_params(("arbitrary", "arbitrary")),
        name="hyena_spectrum",
    )(filt, c2, s2, jnp.asarray(tabs['cphi']), jnp.asarray(tabs['sphi']))


def _hy_conv_kernel(x1_ref, x2_ref, z_ref, w_ref, b_ref, skip_ref, h_ref,
                    ce_ref, co_ref, cet_ref, cot_ref, o_ref, we_ref, wo_ref, stage_ref, *, L):
    H2 = L // 2
    row = lax.broadcasted_iota(jnp.int32, (H2, HY_CT), 0)

    def short_conv(p_ref, i):
        p = p_ref[0].astype(F32)
        for j in range(HY_CT // 128):
            stage_ref[j] = p[:, j * 128:(j + 1) * 128]
        pe = jnp.concatenate([stage_ref[j, pl.ds(0, H2, stride=2), :] for j in range(HY_CT // 128)], axis=1)
        po = jnp.concatenate([stage_ref[j, pl.ds(1, H2, stride=2), :] for j in range(HY_CT // 128)], axis=1)
        w = w_ref[i]
        b = b_ref[i]
        po_prev = jnp.where(row == 0, 0.0, pltpu.roll(po, 1, 0))
        pe_next = jnp.where(row == H2 - 1, 0.0, pltpu.roll(pe, H2 - 1, 0))
        ue = po_prev * w[0:1, :] + pe * w[1:2, :] + po * w[2:3, :] + b
        uo = pe * w[0:1, :] + po * w[1:2, :] + pe_next * w[2:3, :] + b
        return ue, uo

    ft = min(H2, 512)
    ze, zo = short_conv(z_ref, 2)
    for n, gate_ref in enumerate((x1_ref, x2_ref)):
        zeb, zob = ze.astype(BF16), zo.astype(BF16)
        for i in range(H2 // ft):
            lo = slice(i * ft, (i + 1) * ft)
            hi = slice(H2 + i * ft, H2 + (i + 1) * ft)
            a_lo = jnp.dot(ce_ref[lo, :], zeb, preferred_element_type=F32)
            b_lo = jnp.dot(co_ref[lo, :], zob, preferred_element_type=F32)
            a_hi = jnp.dot(ce_ref[hi, :], zeb, preferred_element_type=F32)
            b_hi = jnp.dot(co_ref[hi, :], zob, preferred_element_type=F32)
            ur_lo, us_lo = a_lo + b_lo, a_hi - b_hi
            ur_hi, us_hi = a_hi + b_hi, a_lo - b_lo
            hre, him = h_ref[0, n, lo, :], h_ref[1, n, lo, :]
            yre_lo = ur_lo * hre + us_lo * him
            yim_lo = ur_lo * him - us_lo * hre
            hre, him = h_ref[0, n, hi, :], h_ref[1, n, hi, :]
            yre_hi = ur_hi * hre + us_hi * him
            yim_hi = ur_hi * him - us_hi * hre
            we_ref[lo, :] = (yre_lo - yim_hi).astype(BF16)
            we_ref[hi, :] = (yre_hi - yim_lo).astype(BF16)
            wo_ref[lo, :] = (yre_lo + yim_hi).astype(BF16)
            wo_ref[hi, :] = (yre_hi + yim_lo).astype(BF16)
        ye = jnp.dot(cet_ref[...], we_ref[...], preferred_element_type=F32)
        yo = jnp.dot(cot_ref[...], wo_ref[...], preferred_element_type=F32)
        ge, go = short_conv(gate_ref, n)
        ze = ge * (ye + ze * skip_ref[n])
        zo = go * (yo + zo * skip_ref[n])
    for j in range(HY_CT // 128):
        stage_ref[j, pl.ds(0, H2, stride=2), :] = ze[:, j * 128:(j + 1) * 128]
        stage_ref[j, pl.ds(1, H2, stride=2), :] = zo[:, j * 128:(j + 1) * 128]
        o_ref[0, :, j * 128:(j + 1) * 128] = stage_ref[j].astype(BF16)


def _hy_conv(P, row_blk, L, conv_w, conv_b, skip, spec):
    B = P.shape[0]
    n_ct = HY_WIDTH // HY_CT
    H2 = L // 2
    tabs = _dft_tables(L)
    ce, co, cet, cot = (jnp.asarray(tabs[k]).astype(BF16) for k in ('ce', 'co', 'cet', 'cot'))
    w4 = conv_w.reshape(3, 3, n_ct, HY_CT).transpose(2, 1, 0, 3)
    b4 = conv_b.reshape(3, n_ct, 1, HY_CT).transpose(1, 0, 2, 3)
    sk4 = skip.reshape(2, n_ct, 1, HY_CT).transpose(1, 0, 2, 3)
    h5 = spec.reshape(2, L, 2, n_ct, HY_CT).transpose(3, 0, 2, 1, 4)

    def pcol(off):
        return pl.BlockSpec((1, L, HY_CT), lambda c, b: (b, row_blk, off // HY_CT + c))

    const = lambda c, b: (0, 0)
    return pl.pallas_call(
        functools.partial(_hy_conv_kernel, L=L),
        grid=(n_ct, B),
        in_specs=[pcol(OFF_HY_X1), pcol(OFF_HY_X2), pcol(OFF_HY_ZIN),
                  pl.BlockSpec((None, 3, 3, HY_CT), lambda c, b: (c, 0, 0, 0)),
                  pl.BlockSpec((None, 3, 1, HY_CT), lambda c, b: (c, 0, 0, 0)),
                  pl.BlockSpec((None, 2, 1, HY_CT), lambda c, b: (c, 0, 0, 0)),
                  pl.BlockSpec((None, 2, 2, L, HY_CT), lambda c, b: (c, 0, 0, 0, 0),
                               pipeline_mode=pl.Buffered(1)),
                  _single((L, H2), const), _single((L, H2), const),
                  _single((H2, L), const), _single((H2, L), const)],
        out_specs=pl.BlockSpec((1, L, HY_CT), lambda c, b: (b, 0, c)),
        out_shape=jax.ShapeDtypeStruct((B, L, HY_WIDTH), BF16),
        scratch_shapes=[pltpu.VMEM((L, HY_CT), BF16), pltpu.VMEM((L, HY_CT), BF16),
                        pltpu.VMEM((HY_CT // 128, L, 128), F32)],
        compiler_params=_params(("arbitrary", "arbitrary")),
        name="hyena_conv",
    )(P, P, P, w4, b4, sk4, h5, ce, co, cet, cot)


def _gla_kernel(qk_ref, v_ref, misc_ref, wg_ref, bg_ref, ng_ref, o_ref,
                qe_s, ke_s, qi_s, ko_s, dec_s, o_scr, s_scr):
    C = GLA_CHUNK
    HK = GLA_HEADS * GLA_DK
    NC = GLA_NCHUNK
    logit = jnp.dot(misc_ref[0], wg_ref[...], preferred_element_type=F32) + bg_ref[...]
    g = (jnp.minimum(logit, 0.0) - jnp.log(1.0 + jnp.exp(-jnp.abs(logit)))) * (1.0 / GLA_TAU)
    rmod = lax.broadcasted_iota(jnp.int32, (ROWS, 1), 0) % C
    p = g
    shift = 1
    while shift < C:
        p = p + jnp.where(rmod >= shift, pltpu.roll(p, shift, 0), 0.0)
        shift *= 2
    p3 = p.reshape(NC, C, 2 * HK)
    g3 = g.reshape(NC, C, 2 * HK)
    tot = p3[:, C - 1:C, :]
    q3 = (qk_ref[0, :, 0:HK].astype(F32) * (GLA_DK ** -0.5)).reshape(NC, C, HK)
    k3 = qk_ref[0, :, HK:2 * HK].astype(F32).reshape(NC, C, HK)
    for d in range(2):
        cs = slice(d * HK, (d + 1) * HK)
        if d == 0:
            bc = p3[:, :, cs]
            bm = bc[:, C // 2 - 1:C // 2, :]
        else:
            bc = tot[:, :, cs] - p3[:, :, cs] + g3[:, :, cs]
            bm = bc[:, C // 2:C // 2 + 1, :]
        bl = tot[:, :, cs]
        qe = q3 * jnp.exp(bc - bm)
        ke = k3 * jnp.exp(bm - bc)
        qe_s[d] = qe.reshape(ROWS, HK).astype(BF16)
        ke_s[d] = ke.reshape(ROWS, HK).astype(BF16)
        qi_s[d] = (qe * jnp.exp(bm)).reshape(ROWS, HK).astype(BF16)
        ko_s[d] = (ke * jnp.exp(bl - bm)).reshape(ROWS, HK).astype(BF16)
        dec_s[d] = jnp.broadcast_to(jnp.exp(bl), (NC, 8, HK)).reshape(NC * 8, HK)
    s_scr[...] = jnp.zeros_like(s_scr)

    ri = lax.broadcasted_iota(jnp.int32, (GLA_HEADS * C, C), 0) % C
    ci = lax.broadcasted_iota(jnp.int32, (GLA_HEADS * C, C), 1)
    keep = (ri >= ci, ri <= ci)
    lane = lax.broadcasted_iota(jnp.int32, (1, HK), 1)
    lane2 = lax.broadcasted_iota(jnp.int32, (1, 2 * GLA_DK), 1)

    def intra_scores(d, c):
        rows = pl.ds(pl.multiple_of(c * C, C), C)
        qe = qe_s[d, rows, :]
        stack = jnp.concatenate(
            [jnp.where((lane >= h * GLA_DK) & (lane < (h + 1) * GLA_DK), qe, 0).astype(BF16)
             for h in range(GLA_HEADS)], axis=0)
        a_all = lax.dot_general(stack, ke_s[d, rows, :], (((1,), (1,)), ((), ())),
                                preferred_element_type=F32)
        return jnp.where(keep[d], a_all, 0.0).astype(BF16)

    def state_terms(d, c):
        rows = pl.ds(pl.multiple_of(c * C, C), C)
        qi = qi_s[d, rows, :]
        ko = ko_s[d, rows, :]
        dec = dec_s[d, pl.ds(pl.multiple_of(c * 8, 8), 8), :][0:1, :]
        inter = []
        for h in range(GLA_HEADS):
            sl = slice((h // 2) * 128, (h // 2 + 1) * 128)
            mk = (lane2 >= GLA_DK) if (h % 2) else (lane2 < GLA_DK)
            v_h = v_ref[0, rows, h * GLA_DV:(h + 1) * GLA_DV]
            st = s_scr[d, h]
            qi_h = jnp.where(mk, qi[:, sl], 0).astype(BF16)
            inter.append(lax.dot_general(qi_h, st.astype(BF16), (((1,), (1,)), ((), ())),
                                         preferred_element_type=F32))
            ko_h = jnp.where(mk, ko[:, sl], 0).astype(BF16)
            kv_t = lax.dot_general(v_h, ko_h, (((0,), (0,)), ((), ())), preferred_element_type=F32)
            s_scr[d, h] = st * dec[:, sl] + kv_t
        return inter

    def write_out(d, c, a_all, inter):
        rows = pl.ds(pl.multiple_of(c * C, C), C)
        for h in range(GLA_HEADS):
            v_h = v_ref[0, rows, h * GLA_DV:(h + 1) * GLA_DV]
            o_scr[d, rows, h * GLA_DV:(h + 1) * GLA_DV] = (
                jnp.dot(a_all[h * C:(h + 1) * C], v_h, preferred_element_type=F32) + inter[h])

    def step(i, carry):
        cf = jnp.where(i < GLA_CTX_CHUNKS, GLA_NCHUNK - GLA_CTX_CHUNKS + i, i - GLA_CTX_CHUNKS)
        cb = GLA_NCHUNK - 1 - i
        a_f = intra_scores(0, cf)
        a_b = intra_scores(1, cb)
        i_f = state_terms(0, cf)
        i_b = state_terms(1, cb)
        write_out(0, cf, a_f, i_f)
        write_out(1, cb, a_b, i_b)
        return carry

    lax.fori_loop(0, GLA_NCHUNK, step, 0, unroll=2)

    ng = ng_ref[...]
    for t in range(N_TILES):
        rs = slice(t * TILE, (t + 1) * TILE)
        for h in range(GLA_HEADS):
            cs = slice(h * GLA_DV, (h + 1) * GLA_DV)
            o = o_scr[0, rs, cs] + o_scr[1, rs, cs]
            o = o * lax.rsqrt(jnp.mean(o * o, axis=-1, keepdims=True) + EPS) * ng
            o_ref[0, rs, cs] = o.astype(BF16)


def _gla(P, wg, bg, norm_g):
    B = P.shape[0]
    width = GLA_HEADS * GLA_DV
    const = lambda b: (0, 0)
    return pl.pallas_call(
        _gla_kernel,
        grid=(B,),
        in_specs=[pl.BlockSpec((1, ROWS, 512), lambda b: (b, 0, OFF_GLA_QK // 512)),
                  pl.BlockSpec((1, ROWS, width), lambda b: (b, 0, OFF_GLA_V // width)),
                  pl.BlockSpec((1, ROWS, 128), lambda b: (b, 0, (OFF_MLA + 384) // 128)),
                  pl.BlockSpec((128, 2 * GLA_HEADS * GLA_DK), const),
                  pl.BlockSpec((1, 2 * GLA_HEADS * GLA_DK), const),
                  pl.BlockSpec((1, GLA_DV), const)],
        out_specs=pl.BlockSpec((1, ROWS, width), lambda b: (b, 0, 0)),
        out_shape=jax.ShapeDtypeStruct((B, ROWS, width), BF16),
        scratch_shapes=[pltpu.VMEM((2, ROWS, GLA_HEADS * GLA_DK), BF16),
                        pltpu.VMEM((2, ROWS, GLA_HEADS * GLA_DK), BF16),
                        pltpu.VMEM((2, ROWS, GLA_HEADS * GLA_DK), BF16),
                        pltpu.VMEM((2, ROWS, GLA_HEADS * GLA_DK), BF16),
                        pltpu.VMEM((2, GLA_NCHUNK * 8, GLA_HEADS * GLA_DK), F32),
                        pltpu.VMEM((2, ROWS, width), F32),
                        pltpu.VMEM((2, GLA_HEADS, GLA_DV, 2 * GLA_DK), F32)],
        compiler_params=_params(("arbitrary",)),
        name="gla",
    )(P, P, P, wg, bg, norm_g.reshape(1, -1))


def _merge_kernel(ya_ref, yb_ref, yh_ref, yg_ref, za_ref, zb_ref, zh_ref, zg_ref, gt_ref,
                  x_ref, mod_ref, pg_ref, wb_ref, wo_ref, o_ref):
    merged = None
    one = jnp.asarray(1.0, BF16)
    for i, (y_ref, z_ref) in enumerate(((ya_ref, za_ref), (yb_ref, zb_ref), (yh_ref, zh_ref), (yg_ref, zg_ref))):
        z = z_ref[0]
        t = y_ref[0] * (z * (jnp.tanh(z) + one))
        br = jnp.dot(t, wb_ref[i], preferred_element_type=F32)
        gl = gt_ref[0, :, i * D_MODEL:(i + 1) * D_MODEL].astype(F32)
        term = br * (jnp.tanh(gl) + 1.0)
        merged = term if merged is None else merged + term
    out = jnp.dot(merged.astype(BF16), wo_ref[...], preferred_element_type=F32)
    xn = out * lax.rsqrt(jnp.mean(out * out, axis=-1, keepdims=True) + EPS) * pg_ref[...]
    gate = mod_ref[0][:, 2 * D_MODEL:3 * D_MODEL]
    o_ref[0] = x_ref[0] + gate * xn


def _merge(x, P, ya, yb, yh, yg, mod3, mod_row, post_g, wb, wo, row_off):
    B, R, _ = x.shape
    n_t = R // TILE
    const2 = lambda b, t: (0, 0)

    def uni(width, off):
        return pl.BlockSpec((1, TILE, width), lambda b, t: (b, t + row_off, off // width))

    return pl.pallas_call(
        _merge_kernel,
        grid=(B, n_t),
        in_specs=[uni(512, 0), uni(512, 0),
                  pl.BlockSpec((1, TILE, 512), lambda b, t: (b, t, 0)),
                  uni(512, 0),
                  uni(512, OFF_MLA_Z), uni(512, OFF_NA_Z), uni(512, OFF_HY_Z), uni(512, OFF_GLA_Z),
                  uni(4 * D_MODEL, OFF_MERGE),
                  pl.BlockSpec((1, TILE, D_MODEL), lambda b, t: (b, t, 0)),
                  pl.BlockSpec((1, 1, 3 * D_MODEL), lambda b, t: (mod_row(b), 0, 0)),
                  pl.BlockSpec((1, D_MODEL), const2),
                  pl.BlockSpec((4, 512, D_MODEL), lambda b, t: (0, 0, 0)),
                  pl.BlockSpec((D_MODEL, D_MODEL), const2)],
        out_specs=pl.BlockSpec((1, TILE, D_MODEL), lambda b, t: (b, t, 0)),
        out_shape=jax.ShapeDtypeStruct((B, R, D_MODEL), F32),
        compiler_params=_params(("arbitrary", "arbitrary")),
        name="merge",
    )(ya, yb, yh, yg, P, P, P, P, P, x, mod3, post_g.reshape(1, -1), wb, wo)


def _build_w_big(w):
    def cols(name):
        a, b = _ref_cols(name)
        return w[:, a:b]

    kr = cols('mla_kr')
    krx = kr[:, _ROPE_PERM] * _ROPE_SIGN
    pad32 = jnp.zeros((D_MODEL, 32), w.dtype)
    na = cols('na_qkv')
    nw = NA_HEADS * NA_HEAD_DIM
    na = jnp.concatenate([na[:, :nw] * (NA_HEAD_DIM ** -0.5 * math.log2(math.e)), na[:, nw:]], axis=1)
    parts = [0.5 * cols('merge'), cols('mla_cq'), cols('mla_ckv'), kr, cols('gla_glr'), krx, pad32,
             0.5 * cols('mla_z'), na, 0.5 * cols('na_z'), cols('hy_proj'), 0.5 * cols('hy_z'),
             cols('gla_qk'), cols('gla_v'), 0.5 * cols('gla_z')]
    big = jnp.concatenate(parts, axis=1).astype(BF16)
    assert big.shape == (D_MODEL, P_COLS)
    return big


def _build_mla_weights(q_norm, w_uq, kv_norm, w_ukv):
    dq = MLA_NOPE + MLA_ROPE
    wq3 = w_uq.reshape(MLA_Q_RANK, MLA_HEADS, dq)
    rot = wq3[:, :, MLA_NOPE:]
    rotx = rot[:, :, _ROPE_PERM] * _ROPE_SIGN
    wq = jnp.concatenate([wq3, rotx], axis=-1).reshape(MLA_Q_RANK, MLA_HEADS * MLA_SLOT)
    wkv3 = w_ukv.reshape(MLA_KV_RANK, MLA_HEADS, MLA_NOPE + MLA_V)
    zk = jnp.zeros((MLA_KV_RANK, MLA_HEADS, MLA_SLOT - MLA_NOPE), F32)
    wk = jnp.concatenate([wkv3[:, :, :MLA_NOPE], zk], axis=-1).reshape(MLA_KV_RANK, MLA_HEADS * MLA_SLOT)
    wv3 = wkv3[:, :, MLA_NOPE:]
    zv = jnp.zeros_like(wv3)
    even = jnp.concatenate([wv3, zv], axis=-1)
    odd = jnp.concatenate([zv, wv3], axis=-1)
    sel = (np.arange(MLA_HEADS) % 2 == 1)[None, :, None]
    wv = jnp.where(sel, odd, even).reshape(MLA_KV_RANK, MLA_HEADS * MLA_SLOT)
    e1 = np.zeros((128, MLA_SLOT), np.float32)
    for i in range(MLA_ROPE):
        e1[i, MLA_NOPE + i] = 1.0
        e1[64 + i, MLA_NOPE + MLA_ROPE + i] = 1.0
    return dict(q_norm=q_norm.reshape(1, -1), kv_norm=kv_norm.reshape(1, -1),
                wq=wq.astype(BF16), wk=wk.astype(BF16), wv=wv.astype(BF16),
                e1=jnp.asarray(e1, BF16))


def _build_gla_gate_weights(wg2, bg):
    hk = GLA_HEADS * GLA_DK
    wg = jnp.zeros((128, 2 * hk), F32)
    for i in range(2):
        wg = wg.at[32 + i * GLA_RANK:32 + (i + 1) * GLA_RANK, i * hk:(i + 1) * hk].set(wg2[i])
    return wg.astype(BF16), bg.reshape(1, 2 * hk)


def kernel(x, c, ctx, c_ctx, ada_w, ada_b, pre_g, post_g, w_in, mla_q_norm, mla_w_uq, mla_kv_norm, mla_w_ukv, na_rpb, hy_conv_w, hy_conv_b, hy_pe_w1, hy_pe_b1, hy_pe_freq, hy_pe_w2, hy_pe_b2, hy_pe_w3, hy_skip, gla_wg2, gla_bg, gla_norm, w_branch, w_out):
    B = x.shape[0]
    depth = ada_w.shape[0]
    mod_rows = -(-(B + 1) // 8) * 8
    cc = jnp.zeros((mod_rows, D_MODEL), F32).at[:B].set(c).at[B].set(c_ctx)
    cx = ctx
    for l in range(depth):
        need_ctx = l < depth - 1
        n_tiles = N_TILES if need_ctx else N_LAT_TILES
        mod3 = _adaln(cc, ada_w[l], ada_b[l]).reshape(mod_rows, 1, 3 * D_MODEL)
        P = _inproj(x, cx, mod3, pre_g[l], _build_w_big(w_in[l]))

        mla_w = _build_mla_weights(mla_q_norm[l], mla_w_uq[l], mla_kv_norm[l], mla_w_ukv[l])
        q, k, v = _mla_prep(P, mla_w, N_TILES)
        ya = _mla_attn(q, k, v, n_tiles)

        yb = _na_attn(P, _na_bias_table(na_rpb[l]), n_tiles)

        w1p = jnp.zeros((128, HY_FFN), F32).at[:HY_EMB].set(hy_pe_w1[l])

        def hyena(L, row_blk):
            filt = _hy_filters(L, w1p, hy_pe_b1[l], hy_pe_freq[l], hy_pe_w2[l], hy_pe_b2[l], hy_pe_w3[l])
            spec = _hy_spectrum(filt, L)
            return _hy_conv(P, row_blk, L, hy_conv_w[l], hy_conv_b[l], hy_skip[l], spec)

        yh = hyena(SEQ, 0)

        wg, bgp = _build_gla_gate_weights(gla_wg2[l], gla_bg[l])
        yg = _gla(P, wg, bgp, gla_norm[l])

        wb = (0.5 * w_branch[l]).astype(BF16)
        wo = w_out[l].astype(BF16)
        if need_ctx:
            yhc = hyena(CTX_LEN, SEQ // CTX_LEN)
            cx = _merge(cx, P, ya, yb, yhc, yg, mod3, lambda b: B, post_g[l], wb, wo, N_LAT_TILES)
        x = _merge(x, P, ya, yb, yh, yg, mod3, lambda b: b, post_g[l], wb, wo, 0)
    return x
```
